```python
import math, functools
import jax, jax.numpy as jnp
from jax import lax
import numpy as np

D_MODEL = 1024
BATCH = 4
SEQ = 4096
DEPTH = 1
DEC_BATCH = 128
DEC_SEQ = 4
PAST_LEN = 8192
PAGE_SIZE = 128

D_MIX = D_MODEL
RWKV_WIDTH = D_MIX // 2
RWKV_HD = 64
N_RWKV_HEADS = RWKV_WIDTH // RWKV_HD
DECAY_LORA = 64
AAA_LORA = 64
GATE_LORA = 128
RWKV_COLS = 3 * RWKV_WIDTH + DECAY_LORA + AAA_LORA + GATE_LORA
RWKV_SPLITS = (RWKV_WIDTH, 2 * RWKV_WIDTH, 3 * RWKV_WIDTH, 3 * RWKV_WIDTH + DECAY_LORA,
               3 * RWKV_WIDTH + DECAY_LORA + AAA_LORA)
GN_EPS = 64e-5
DIFF_WIDTH = D_MIX - RWKV_WIDTH
DIFF_HD = 64
DIFF_VD = 2 * DIFF_HD
N_DIFF_HEADS = DIFF_WIDTH // DIFF_VD
DIFF_QK = N_DIFF_HEADS * 2 * DIFF_HD
IN_COLS = RWKV_COLS + 2 * DIFF_QK + DIFF_WIDTH
Q_BLOCK = 128
RMS_EPS = 1e-5
N_BUCKETS = 32
MAX_DISTANCE = 128
N_MEM = 256
N_MEM_HEADS = 4
MEM_HD = D_MODEL // N_MEM_HEADS
N_EXPERTS = 64
N_GROUPS = 8
TOPK_GROUPS = 4
TOP_K = 8
D_EXPERT = 256
D_SHARED = 256
ROUTED_SCALE = 2.5
MOE_BLOCK = 512
DN_ALPHA = (2.0 * DEPTH) ** 0.25
DN_BETA = (8.0 * DEPTH) ** -0.25
LN_EPS = 1e-5
NEG_INF = -1e30

kernel_name = "hybrid_rwkv7_diffattn_moe_decode_step"


def layer_norm(x, g, b):
    xf = x.astype(jnp.float32)
    mu = jnp.mean(xf, -1, keepdims=True)
    var = jnp.mean(jnp.square(xf - mu), -1, keepdims=True)
    y = (xf - mu) * lax.rsqrt(var + LN_EPS) * g.astype(jnp.float32) + b.astype(jnp.float32)
    return y.astype(x.dtype)


def t5_bucket(dist):
    max_exact = N_BUCKETS // 2
    d = jnp.maximum(dist, 1).astype(jnp.float32)
    large = max_exact + (jnp.log(d / max_exact) / math.log(MAX_DISTANCE / max_exact)
                         * (N_BUCKETS - max_exact)).astype(jnp.int32)
    return jnp.where(dist < max_exact, dist, jnp.minimum(large, N_BUCKETS - 1))


def rel_bias_and_mask(rel_bias, q_pos, k_pos):
    dist = q_pos[:, None] - k_pos[None, :]
    bias = rel_bias[t5_bucket(jnp.maximum(dist, 0))]
    return jnp.transpose(bias, (2, 3, 0, 1)).astype(jnp.float32), dist >= 0


def diff_attend(q, k, v, bias, mask, lam):
    logits = jnp.einsum('bqhmd,bkhmd->bhmqk', q, k).astype(jnp.float32) * DIFF_HD ** -0.5 + bias
    logits = jnp.where(mask, logits, NEG_INF)
    p = jax.nn.softmax(logits, axis=-1)
    attn = p[:, :, 0] - lam * p[:, :, 1]
    return jnp.einsum('bhqk,bkhv->bqhv', attn.astype(v.dtype), v)


def diff_attn_prompt(q, k, v, lam, rel_bias):
    B, T = q.shape[0], q.shape[1]
    nb = T // Q_BLOCK
    k_pos = jnp.arange(T, dtype=jnp.int32)
    q_blocks = jnp.moveaxis(q.reshape(B, nb, Q_BLOCK, N_DIFF_HEADS, 2, DIFF_HD), 1, 0)

    def one_block(args):
        i, q_i = args
        q_pos = i * Q_BLOCK + jnp.arange(Q_BLOCK, dtype=jnp.int32)
        bias, mask = rel_bias_and_mask(rel_bias, q_pos, k_pos)
        return diff_attend(q_i, k, v, bias, mask, lam)

    o = lax.map(one_block, (jnp.arange(nb, dtype=jnp.int32), q_blocks))
    return jnp.moveaxis(o, 0, 1).reshape(B, T, N_DIFF_HEADS, DIFF_VD)


def diff_attn_sample(q, k, v, lam, cache_k, cache_v, layer, page_table, rel_bias):
    S = q.shape[1]
    past = page_table.shape[1] * PAGE_SIZE
    q_pos = past + jnp.arange(S, dtype=jnp.int32)
    k_pos = jnp.arange(past + S, dtype=jnp.int32)
    bias, mask = rel_bias_and_mask(rel_bias, q_pos, k_pos)

    def one_seq(args):
        q_b, k_b, v_b, pages = args
        k_past = cache_k[layer, pages].reshape(past, N_DIFF_HEADS, 2, DIFF_HD)
        v_past = cache_v[layer, pages].reshape(past, N_DIFF_HEADS, DIFF_VD)
        keys = jnp.concatenate([k_past.astype(k_b.dtype), k_b], axis=0)[None]
        vals = jnp.concatenate([v_past.astype(v_b.dtype), v_b], axis=0)[None]
        return diff_attend(q_b[None], keys, vals, bias, mask, lam)[0]

    return lax.map(one_seq, (q, k, v, page_table))


def diff_post(o, subln_g, lam_init):
    of = o.astype(jnp.float32)
    of = of * lax.rsqrt(jnp.mean(of * of, -1, keepdims=True) + RMS_EPS) * subln_g * (1.0 - lam_init)
    return of.reshape(o.shape[0], o.shape[1], DIFF_WIDTH).astype(o.dtype)


def wkv_scan(r, w, k, v, a, b, s0):
    def step(s, xs):
        r_t, w_t, k_t, v_t, a_t, b_t = xs
        sa = jnp.einsum('bhij,bhj->bhi', s, a_t)
        s = s * w_t[:, :, None, :] + sa[..., None] * b_t[:, :, None, :] + v_t[..., None] * k_t[:, :, None, :]
        return s, jnp.einsum('bhij,bhj->bhi', s, r_t)

    xs = tuple(jnp.swapaxes(t.astype(jnp.float32), 0, 1) for t in (r, w, k, v, a, b))
    s_last, ys = lax.scan(step, s0.astype(jnp.float32), xs)
    return jnp.swapaxes(ys, 0, 1), s_last


def rwkv_group(p, prev, s0, lp):
    B, T = p.shape[0], p.shape[1]
    p_prev = jnp.concatenate([prev[:, None, :].astype(p.dtype), p[:, :-1]], axis=1)
    ps = p + (p_prev - p) * lp['mu_shift']
    r, k, v, w_lo, a_lo, g_lo = jnp.split(ps, RWKV_SPLITS, axis=-1)
    w_log = -jax.nn.softplus(-(lp['w0'] + jnp.tanh(w_lo) @ lp['w_decay_up']).astype(jnp.float32)) - 0.5
    decay = jnp.exp(-jnp.exp(w_log))
    a = jax.nn.sigmoid(lp['a0'] + a_lo @ lp['w_aaa_up'])
    g = jax.nn.sigmoid(g_lo) @ lp['w_gate_up']

    def heads(t):
        return t.reshape(B, T, N_RWKV_HEADS, RWKV_HD)

    kk = heads(k * lp['k_k']).astype(jnp.float32)
    kk = kk / jnp.maximum(jnp.sqrt(jnp.sum(kk * kk, -1, keepdims=True)), 1e-12)
    k = k * (1.0 + (a - 1.0) * lp['k_a'])
    rh, kh, vh, ah = heads(r), heads(k), heads(v), heads(a)
    y, s_new = wkv_scan(rh, heads(decay), kh, vh, -kk, kk * ah, s0)
    mu = jnp.mean(y, -1, keepdims=True)
    var = jnp.mean(jnp.square(y - mu), -1, keepdims=True)
    yn = ((y - mu) * lax.rsqrt(var + GN_EPS)).reshape(B, T, RWKV_WIDTH) * lp['lnx_g'] + lp['lnx_b']
    bonus = (jnp.sum(rh * kh * lp['r_k'], -1, keepdims=True) * vh).reshape(B, T, RWKV_WIDTH)
    out = (yn + bonus) * g
    return out.astype(p.dtype), s_new, p[:, -1]


def mem_kv(mem, w_mk, w_mv):
    B, M = mem.shape[0], mem.shape[1]
    return ((mem @ w_mk).reshape(B, M, N_MEM_HEADS, MEM_HD),
            (mem @ w_mv).reshape(B, M, N_MEM_HEADS, MEM_HD))


def mem_attend(h, mk, mv, w_mq, w_mo):
    B, T = h.shape[0], h.shape[1]
    q = (h @ w_mq).reshape(B, T, N_MEM_HEADS, MEM_HD)
    logits = jnp.einsum('bqhd,bmhd->bhqm', q, mk.astype(q.dtype)).astype(jnp.float32) * MEM_HD ** -0.5
    p = jax.nn.softmax(logits, axis=-1)
    o = jnp.einsum('bhqm,bmhd->bqhd', p.astype(h.dtype), mv.astype(h.dtype)).reshape(B, T, D_MODEL)
    return o @ w_mo


def moe_ffn(h, lp):
    shape = h.shape
    xt = h.reshape(-1, D_MODEL)
    n = xt.shape[0]
    n_pad = -(-n // MOE_BLOCK) * MOE_BLOCK
    xt = jnp.pad(xt, ((0, n_pad - n), (0, 0)))

    def block(xb):
        scores = jax.nn.sigmoid((xb @ lp['w_router']).astype(jnp.float32))
        biased = scores + lp['e_bias'].astype(jnp.float32)
        grp = lax.top_k(biased.reshape(-1, N_GROUPS, N_EXPERTS // N_GROUPS), 2)[0].sum(-1)
        _, gidx = lax.top_k(grp, TOPK_GROUPS)
        gmask = jax.nn.one_hot(gidx, N_GROUPS, dtype=jnp.float32).sum(1)
        emask = jnp.repeat(gmask, N_EXPERTS // N_GROUPS, axis=-1) > 0
        _, eidx = lax.top_k(jnp.where(emask, biased, NEG_INF), TOP_K)
        sel = jnp.take_along_axis(scores, eidx, axis=-1)
        wts = sel / jnp.sum(sel, -1, keepdims=True) * ROUTED_SCALE
        gates = jnp.einsum('nk,nke->ne', wts, jax.nn.one_hot(eidx, N_EXPERTS, dtype=jnp.float32))
        hg = jnp.einsum('nd,edf->nef', xb, lp['w_e_gate'])
        hu = jnp.einsum('nd,edf->nef', xb, lp['w_e_up'])
        act = jax.nn.silu(hg) * hu * gates[..., None]
        routed = jnp.einsum('nef,efd->nd', act, lp['w_e_down'])
        shared = (jax.nn.silu(xb @ lp['w_s_gate']) * (xb @ lp['w_s_up'])) @ lp['w_s_down']
        return (routed + shared).astype(xb.dtype)

    out = lax.map(block, xt.reshape(n_pad // MOE_BLOCK, MOE_BLOCK, D_MODEL))
    return out.reshape(n_pad, D_MODEL)[:n].reshape(shape)


def trunk_layer(h, shift_prev, s0, attend, mk, mv, lp, lam_init):
    B, T = h.shape[0], h.shape[1]
    proj = h @ lp['w_in']
    y_rw, s_new, shift_new = rwkv_group(proj[..., :RWKV_COLS], shift_prev, s0, lp)
    pd = proj[..., RWKV_COLS:]
    q = pd[..., :DIFF_QK].reshape(B, T, N_DIFF_HEADS, 2, DIFF_HD)
    k = pd[..., DIFF_QK:2 * DIFF_QK].reshape(B, T, N_DIFF_HEADS, 2, DIFF_HD)
    v = pd[..., 2 * DIFF_QK:].reshape(B, T, N_DIFF_HEADS, DIFF_VD)
    lam = (jnp.exp(jnp.sum(lp['lam_q1'] * lp['lam_k1']).astype(jnp.float32))
           - jnp.exp(jnp.sum(lp['lam_q2'] * lp['lam_k2']).astype(jnp.float32)) + lam_init)
    y_df = diff_post(attend(q, k, v, lam), lp['subln_g'], lam_init)
    mix = jnp.concatenate([y_rw, y_df], axis=-1) @ lp['w_out']
    h1 = layer_norm(DN_ALPHA * h + mix, lp['ln1_g'], lp['ln1_b'])
    h2 = layer_norm(DN_ALPHA * h1 + mem_attend(h1, mk, mv, lp['w_mq'], lp['w_mo']), lp['ln2_g'], lp['ln2_b'])
    h3 = layer_norm(DN_ALPHA * h2 + moe_ffn(h2, lp), lp['ln3_g'], lp['ln3_b'])
    return h3, s_new, shift_new, k, v


def setup_inputs(seed: int = 0) -> dict:
    key = jax.random.key(seed)
    ks = iter(jax.random.split(key, 96))

    def nrm(shape, scale):
        return jax.random.normal(next(ks), shape, jnp.float32) * scale

    L = DEPTH
    n_pages = PAST_LEN // PAGE_SIZE
    n_used = DEC_BATCH * n_pages
    n_pool = (n_used * 5 + 3) // 4
    page_table = jax.random.permutation(next(ks), n_pool)[:n_used].reshape(DEC_BATCH, n_pages).astype(jnp.int32)
    ch = jnp.arange(RWKV_WIDTH, dtype=jnp.float32) / (RWKV_WIDTH - 1)
    return {
        'x_prompt': nrm((BATCH, SEQ, D_MODEL), 1.0),
        'x_sample': nrm((DEC_BATCH, DEC_SEQ, D_MODEL), 1.0),
        'cache_k': nrm((L, n_pool, PAGE_SIZE, N_DIFF_HEADS, 2, DIFF_HD), 1.0),
        'cache_v': nrm((L, n_pool, PAGE_SIZE, N_DIFF_HEADS, DIFF_VD), 1.0),
        'state_wkv': nrm((L, DEC_BATCH, N_RWKV_HEADS, RWKV_HD, RWKV_HD), 0.3),
        'state_shift': nrm((L, DEC_BATCH, RWKV_COLS), 1.0),
        'cache_mem_k': nrm((L, DEC_BATCH, N_MEM, N_MEM_HEADS, MEM_HD), 1.0),
        'cache_mem_v': nrm((L, DEC_BATCH, N_MEM, N_MEM_HEADS, MEM_HD), 1.0),
        'page_table': page_table,
        'mem_prompt': nrm((BATCH, N_MEM, D_MODEL), 1.0),
        'ln_in_g': 1.0 + nrm((D_MODEL,), 0.02),
        'ln_in_b': nrm((D_MODEL,), 0.02),
        'rel_bias': nrm((N_BUCKETS, N_DIFF_HEADS, 2), 0.3),
        'w_in': nrm((L, D_MODEL, IN_COLS), D_MODEL ** -0.5),
        'mu_shift': jax.random.uniform(next(ks), (L, RWKV_COLS), jnp.float32),
        'w0': -6.5 + 5.0 * ch ** 0.85 + nrm((L, RWKV_WIDTH), 0.1),
        'w_decay_up': nrm((L, DECAY_LORA, RWKV_WIDTH), DECAY_LORA ** -0.5),
        'a0': nrm((L, RWKV_WIDTH), 0.1),
        'w_aaa_up': nrm((L, AAA_LORA, RWKV_WIDTH), AAA_LORA ** -0.5),
        'w_gate_up': nrm((L, GATE_LORA, RWKV_WIDTH), GATE_LORA ** -0.5),
        'k_k': 0.85 + nrm((L, RWKV_WIDTH), 0.02),
        'k_a': 1.0 + nrm((L, RWKV_WIDTH), 0.02),
        'r_k': nrm((L, N_RWKV_HEADS, RWKV_HD), 0.1),
        'lnx_g': 1.0 + nrm((L, RWKV_WIDTH), 0.02),
        'lnx_b': nrm((L, RWKV_WIDTH), 0.02),
        'lam_q1': nrm((L, DIFF_HD), 0.1),
        'lam_k1': nrm((L, DIFF_HD), 0.1),
        'lam_q2': nrm((L, DIFF_HD), 0.1),
        'lam_k2': nrm((L, DIFF_HD), 0.1),
        'subln_g': 1.0 + nrm((L, DIFF_VD), 0.02),
        'w_out': nrm((L, D_MIX, D_MODEL), DN_BETA * D_MIX ** -0.5),
        'ln1_g': 1.0 + nrm((L, D_MODEL), 0.02),
        'ln1_b': nrm((L, D_MODEL), 0.02),
        'w_mq': nrm((L, D_MODEL, D_MODEL), D_MODEL ** -0.5),
        'w_mk': nrm((L, D_MODEL, D_MODEL), D_MODEL ** -0.5),
        'w_mv': nrm((L, D_MODEL, D_MODEL), D_MODEL ** -0.5),
        'w_mo': nrm((L, D_MODEL, D_MODEL), DN_BETA * D_MODEL ** -0.5),
        'ln2_g': 1.0 + nrm((L, D_MODEL), 0.02),
        'ln2_b': nrm((L, D_MODEL), 0.02),
        'w_router': nrm((L, D_MODEL, N_EXPERTS), D_MODEL ** -0.5),
        'e_bias': nrm((L, N_EXPERTS), 0.01),
        'w_e_gate': nrm((L, N_EXPERTS, D_MODEL, D_EXPERT), D_MODEL ** -0.5),
        'w_e_up': nrm((L, N_EXPERTS, D_MODEL, D_EXPERT), D_MODEL ** -0.5),
        'w_e_down': nrm((L, N_EXPERTS, D_EXPERT, D_MODEL), DN_BETA * D_EXPERT ** -0.5),
        'w_s_gate': nrm((L, D_MODEL, D_SHARED), D_MODEL ** -0.5),
        'w_s_up': nrm((L, D_MODEL, D_SHARED), D_MODEL ** -0.5),
        'w_s_down': nrm((L, D_SHARED, D_MODEL), DN_BETA * D_SHARED ** -0.5),
        'ln3_g': 1.0 + nrm((L, D_MODEL), 0.02),
        'ln3_b': nrm((L, D_MODEL), 0.02),
    }


def reference(x_prompt, x_sample, cache_k, cache_v, state_wkv, state_shift, cache_mem_k, cache_mem_v,
              page_table, mem_prompt, ln_in_g, ln_in_b, rel_bias, w_in, mu_shift, w0, w_decay_up, a0,
              w_aaa_up, w_gate_up, k_k, k_a, r_k, lnx_g, lnx_b, lam_q1, lam_k1, lam_q2, lam_k2, subln_g,
              w_out, ln1_g, ln1_b, w_mq, w_mk, w_mv, w_mo, ln2_g, ln2_b, w_router, e_bias, w_e_gate,
              w_e_up, w_e_down, w_s_gate, w_s_up, w_s_down, ln3_g, ln3_b):
    hp = layer_norm(x_prompt, ln_in_g, ln_in_b)
    hs = layer_norm(x_sample, ln_in_g, ln_in_b)
    B = hp.shape[0]
    new = {n: [] for n in ('k_p', 'v_p', 'wkv_p', 'shift_p', 'mk_p', 'mv_p', 'k_s', 'v_s', 'wkv_s', 'shift_s')}
    prompt_attend = functools.partial(diff_attn_prompt, rel_bias=rel_bias)
    for l in range(DEPTH):
        lp = dict(w_in=w_in[l], mu_shift=mu_shift[l], w0=w0[l], w_decay_up=w_decay_up[l], a0=a0[l],
                  w_aaa_up=w_aaa_up[l], w_gate_up=w_gate_up[l], k_k=k_k[l], k_a=k_a[l], r_k=r_k[l],
                  lnx_g=lnx_g[l], lnx_b=lnx_b[l], lam_q1=lam_q1[l], lam_k1=lam_k1[l], lam_q2=lam_q2[l],
                  lam_k2=lam_k2[l], subln_g=subln_g[l], w_out=w_out[l], ln1_g=ln1_g[l], ln1_b=ln1_b[l],
                  w_mq=w_mq[l], w_mo=w_mo[l], ln2_g=ln2_g[l], ln2_b=ln2_b[l], w_router=w_router[l],
                  e_bias=e_bias[l], w_e_gate=w_e_gate[l], w_e_up=w_e_up[l], w_e_down=w_e_down[l],
                  w_s_gate=w_s_gate[l], w_s_up=w_s_up[l], w_s_down=w_s_down[l], ln3_g=ln3_g[l], ln3_b=ln3_b[l])
        lam_init = 0.8 - 0.6 * math.exp(-0.3 * l)
        mk_p, mv_p = mem_kv(mem_prompt, w_mk[l], w_mv[l])
        hp, s_p, sh_p, k_p, v_p = trunk_layer(
            hp, jnp.zeros((B, RWKV_COLS), hp.dtype),
            jnp.zeros((B, N_RWKV_HEADS, RWKV_HD, RWKV_HD), jnp.float32),
            prompt_attend, mk_p, mv_p, lp, lam_init)
        sample_attend = functools.partial(diff_attn_sample, cache_k=cache_k, cache_v=cache_v, layer=l,
                                          page_table=page_table, rel_bias=rel_bias)
        hs, s_s, sh_s, k_s, v_s = trunk_layer(
            hs, state_shift[l], state_wkv[l], sample_attend, cache_mem_k[l], cache_mem_v[l], lp, lam_init)
        new['k_p'].append(k_p)
        new['v_p'].append(v_p)
        new['wkv_p'].append(s_p)
        new['shift_p'].append(sh_p)
        new['mk_p'].append(mk_p)
        new['mv_p'].append(mv_p)
        new['k_s'].append(k_s)
        new['v_s'].append(v_s)
        new['wkv_s'].append(s_s)
        new['shift_s'].append(sh_s)
    return (hp, hs,
            jnp.stack(new['k_p']), jnp.stack(new['v_p']), jnp.stack(new['wkv_p']), jnp.stack(new['shift_p']),
            jnp.stack(new['mk_p']), jnp.stack(new['mv_p']),
            jnp.stack(new['k_s']), jnp.stack(new['v_s']), jnp.stack(new['wkv_s']), jnp.stack(new['shift_s']))
```

```python
import functools
import math

import jax
import jax.numpy as jnp
from jax import lax
from jax.experimental import pallas as pl
from jax.experimental.pallas import tpu as pltpu

F32 = jnp.float32
BF16 = jnp.bfloat16

D_MODEL = 1024
RWKV_WIDTH = 512
RWKV_HD = 64
N_RWKV_HEADS = 8
RWKV_COLS = 1792
GN_EPS = 64e-5
DIFF_WIDTH = 512
DIFF_HD = 64
DIFF_VD = 128
N_DIFF_HEADS = 4
RMS_EPS = 1e-5
N_BUCKETS = 32
MAX_DISTANCE = 128
PAGE_SIZE = 128
N_MEM = 256
N_MEM_HEADS = 4
MEM_HD = 256
N_EXPERTS = 64
N_GROUPS = 8
GROUP_SIZE = 8
TOPK_GROUPS = 4
TOP_K = 8
D_EXPERT = 256
ROUTED_SCALE = 2.5
DEPTH = 1
DN_ALPHA = (2.0 * DEPTH) ** 0.25
LN_EPS = 1e-5
NEG_INF = -1e30
LAM_INIT = 0.8 - 0.6 * math.exp(-0.3 * 0)

LANES = 128
SUBLANES = 8
VMEM_LIMIT = 56 * 1024 * 1024

SCAN_JLO = 16
SCAN_JHI = RWKV_HD // SCAN_JLO
SCAN_BH = LANES // SCAN_JHI


def _cparams(sem):
    return pltpu.CompilerParams(dimension_semantics=sem, vmem_limit_bytes=VMEM_LIMIT)


def _layer_norm(x, g, b):
    mu = jnp.mean(x, -1, keepdims=True)
    d = x - mu
    var = jnp.mean(d * d, -1, keepdims=True)
    return d * lax.rsqrt(var + LN_EPS) * g + b


def _dot(a, b):
    return jnp.dot(a, b, preferred_element_type=F32)


def _dot_nt(a, b):
    return lax.dot_general(a, b, (((1,), (1,)), ((), ())), preferred_element_type=F32)


def _seg_sum(x, ones_blk):
    outs = []
    for c in range(x.shape[1] // LANES):
        xc = x[:, c * LANES:(c + 1) * LANES]
        hi = xc.astype(BF16)
        lo = (xc - hi.astype(F32)).astype(BF16)
        outs.append(_dot(hi, ones_blk) + _dot(lo, ones_blk))
    return outs[0] if len(outs) == 1 else jnp.concatenate(outs, axis=1)


def _ln_proj_body(x_ref, g_ref, b_ref, w_ref, h_ref, p_ref, q_ref, k_ref, v_ref):
    h = _layer_norm(x_ref[...], g_ref[...], b_ref[...])
    h_ref[...] = h
    hb = h.astype(BF16)
    c0 = RWKV_COLS
    p_ref[...] = _dot(hb, w_ref[:, 0:c0])
    q_ref[...] = _dot(hb, w_ref[:, c0:c0 + 512])
    k_ref[...] = _dot(hb, w_ref[:, c0 + 512:c0 + 1024])
    v_ref[...] = _dot(hb, w_ref[:, c0 + 1024:c0 + 1536])


def ln_proj(x, g, b, w_bf16, tm):
    n = x.shape[0]
    row = lambda width: pl.BlockSpec((tm, width), lambda i: (i, 0))
    full = lambda a: pl.BlockSpec(a.shape, lambda i: (0,) * a.ndim)
    return pl.pallas_call(
        _ln_proj_body,
        grid=(n // tm,),
        in_specs=[row(D_MODEL), full(g), full(b), full(w_bf16)],
        out_specs=[row(D_MODEL), row(RWKV_COLS), row(512), row(512), row(512)],
        out_shape=[jax.ShapeDtypeStruct((n, wd), F32) for wd in (D_MODEL, RWKV_COLS, 512, 512, 512)],
        compiler_params=_cparams(("arbitrary",)),
        name="ln_proj",
    )(x, g, b, w_bf16)


def _mm_body(x_ref, w_ref, o_ref):
    o_ref[...] = _dot(x_ref[...].astype(BF16), w_ref[...])


def matmul(x, w_bf16, tm, name):
    n, kdim = x.shape
    nout = w_bf16.shape[1]
    return pl.pallas_call(
        _mm_body,
        grid=(n // tm,),
        in_specs=[pl.BlockSpec((tm, kdim), lambda i: (i, 0)), pl.BlockSpec((kdim, nout), lambda i: (0, 0))],
        out_specs=pl.BlockSpec((tm, nout), lambda i: (i, 0)),
        out_shape=jax.ShapeDtypeStruct((n, nout), F32),
        compiler_params=_cparams(("arbitrary",)),
        name=name,
    )(x, w_bf16)


def _mm_res_ln_body(x_ref, res_ref, w_ref, g_ref, b_ref, o_ref):
    y = _dot(x_ref[...].astype(BF16), w_ref[...])
    o_ref[...] = _layer_norm(DN_ALPHA * res_ref[...] + y, g_ref[...], b_ref[...])


def matmul_res_ln(x, res, w_bf16, g, b, tm, name):
    n, kdim = x.shape
    nout = w_bf16.shape[1]
    row = lambda width: pl.BlockSpec((tm, width), lambda i: (i, 0))
    full = lambda a: pl.BlockSpec(a.shape, lambda i: (0,) * a.ndim)
    return pl.pallas_call(
        _mm_res_ln_body,
        grid=(n // tm,),
        in_specs=[row(kdim), row(nout), full(w_bf16), full(g), full(b)],
        out_specs=row(nout),
        out_shape=jax.ShapeDtypeStruct((n, nout), F32),
        compiler_params=_cparams(("arbitrary",)),
        name=name,
    )(x, res, w_bf16, g, b)


def _rwkv_prep_body(shift_in_kernel, seq_blocks, p_ref, prev_ref, mu_ref, wlo_ref, w0a0_ref, wg_ref,
                    kk_ref, ka_ref, rk_ref, seg_ref,
                    a_o, w_o, wr_o, b_o, k_o, v_o, br_o, kr_o, g_o, bv_o):
    p = p_ref[...]
    if shift_in_kernel:
        first = jnp.where(pl.program_id(0) % seq_blocks == 0, 0.0, prev_ref[SUBLANES - 1:SUBLANES, :])
        row = lax.broadcasted_iota(jnp.int32, p.shape, 0)
        p_prev = jnp.where(row == 0, first, pltpu.roll(p, 1, 0))
    else:
        p_prev = prev_ref[...]
    ps = p + (p_prev - p) * mu_ref[...]
    r = ps[:, 0:512]
    k = ps[:, 512:1024]
    v = ps[:, 1024:1536]
    lo = ps[:, 1536:1664]
    g_lo = ps[:, 1664:1792]
    lane = lax.broadcasted_iota(jnp.int32, lo.shape, 1)
    lo = jnp.where(lane < 64, jnp.tanh(lo), lo)
    wa = _dot(lo.astype(BF16), wlo_ref[...]) + w0a0_ref[...]
    x = -wa[:, 0:512]
    softplus = jnp.maximum(x, 0.0) + jnp.log(1.0 + jnp.exp(-jnp.abs(x)))
    decay = jnp.exp(-jnp.exp(-softplus - 0.5))
    a = jax.nn.sigmoid(wa[:, 512:1024])
    g = _dot(jax.nn.sigmoid(g_lo).astype(BF16), wg_ref[...])
    seg = seg_ref[...]
    kk = k * kk_ref[...]
    kk = kk / jnp.maximum(jnp.sqrt(_seg_sum(kk * kk, seg)), 1e-12)
    k2 = k * (1.0 + (a - 1.0) * ka_ref[...])
    b = kk * a
    a_o[...] = -kk
    w_o[...] = decay
    wr_o[...] = decay * r
    b_o[...] = b
    k_o[...] = k2
    v_o[...] = v
    br_o[...] = _seg_sum(b * r, seg)
    kr_o[...] = _seg_sum(k2 * r, seg)
    g_o[...] = g
    bv_o[...] = _seg_sum(r * k2 * rk_ref[...], seg) * v


def rwkv_prep(p, prev, consts, tm, shift_in_kernel, seq_len):
    n = p.shape[0]
    row = lambda width: pl.BlockSpec((tm, width), lambda i: (i, 0))
    full = lambda a: pl.BlockSpec(a.shape, lambda i: (0,) * a.ndim)
    if shift_in_kernel:
        per8 = tm // SUBLANES
        prev_spec = pl.BlockSpec((SUBLANES, RWKV_COLS), lambda i: (jnp.maximum(i * per8 - 1, 0), 0))
        prev = p
    else:
        prev_spec = row(RWKV_COLS)
    body = functools.partial(_rwkv_prep_body, shift_in_kernel, max(seq_len // tm, 1))
    return pl.pallas_call(
        body,
        grid=(n // tm,),
        in_specs=[row(RWKV_COLS), prev_spec] + [full(c) for c in consts],
        out_specs=[row(512)] * 10,
        out_shape=[jax.ShapeDtypeStruct((n, 512), F32)] * 10,
        compiler_params=_cparams(("arbitrary",)),
        name="rwkv_prep",
    )(p, prev, *consts)


def _lane_fold(x):
    return x + pltpu.roll(x, 32, 1) + pltpu.roll(x, 64, 1) + pltpu.roll(x, 96, 1)


def _wkv_scan_body(tc, a_ref, w_ref, wr_ref, b_ref, k_ref, v_ref, br_ref, kr_ref, s0_ref,
                   y_ref, st_ref, s_scr):
    c = pl.program_id(1)
    ngrp = RWKV_HD // SUBLANES

    @pl.when(c == 0)
    def _():
        s_scr[...] = s0_ref[0]

    def bcast(ref, t, j):
        return jnp.broadcast_to(ref[0, t, pl.ds(j, 1), :], (SUBLANES, LANES))

    def step(t, carry):
        acc_a = [jnp.zeros((SUBLANES, LANES), F32) for _ in range(ngrp)]
        acc_y = [jnp.zeros((SUBLANES, LANES), F32) for _ in range(ngrp)]
        for j in range(SCAN_JLO):
            ab = bcast(a_ref, t, j)
            wrb = bcast(wr_ref, t, j)
            for ig in range(ngrp):
                s = s_scr[j, ig * SUBLANES:(ig + 1) * SUBLANES, :]
                acc_a[ig] = acc_a[ig] + s * ab
                acc_y[ig] = acc_y[ig] + s * wrb
        sa = [_lane_fold(x) for x in acc_a]
        vv = [v_ref[0, t, ig * SUBLANES:(ig + 1) * SUBLANES, :] for ig in range(ngrp)]
        for j in range(SCAN_JLO):
            wb = bcast(w_ref, t, j)
            bb = bcast(b_ref, t, j)
            kb = bcast(k_ref, t, j)
            for ig in range(ngrp):
                sl = slice(ig * SUBLANES, (ig + 1) * SUBLANES)
                s_scr[j, sl, :] = s_scr[j, sl, :] * wb + sa[ig] * bb + vv[ig] * kb
        br = jnp.broadcast_to(br_ref[0, t], (SUBLANES, LANES))
        kr = jnp.broadcast_to(kr_ref[0, t], (SUBLANES, LANES))
        for ig in range(ngrp):
            y_ref[0, t, ig * SUBLANES:(ig + 1) * SUBLANES, :] = _lane_fold(acc_y[ig]) + sa[ig] * br + vv[ig] * kr
        return carry

    lax.fori_loop(0, tc, step, 0)

    @pl.when(c == pl.num_programs(1) - 1)
    def _():
        st_ref[0] = s_scr[...]


def wkv_scan(a, w, wr, b, k, v, br, kr, s0, tc):
    g, t = a.shape[0], a.shape[1]
    jspec = pl.BlockSpec((1, tc, SCAN_JLO, LANES), lambda gi, c: (gi, c, 0, 0))
    ispec = pl.BlockSpec((1, tc, RWKV_HD, LANES), lambda gi, c: (gi, c, 0, 0))
    sspec = pl.BlockSpec((1, tc, 1, LANES), lambda gi, c: (gi, c, 0, 0))
    stspec = pl.BlockSpec((1, SCAN_JLO, RWKV_HD, LANES), lambda gi, c: (gi, 0, 0, 0))
    return pl.pallas_call(
        functools.partial(_wkv_scan_body, tc),
        grid=(g, t // tc),
        in_specs=[jspec] * 5 + [ispec, sspec, sspec, stspec],
        out_specs=[ispec, stspec],
        out_shape=[jax.ShapeDtypeStruct((g, t, RWKV_HD, LANES), F32),
                   jax.ShapeDtypeStruct((g, SCAN_JLO, RWKV_HD, LANES), F32)],
        scratch_shapes=[pltpu.VMEM((SCAN_JLO, RWKV_HD, LANES), F32)],
        compiler_params=_cparams(("arbitrary", "arbitrary")),
        name="wkv_scan",
    )(a, w, wr, b, k, v, br, kr, s0)


def _to_scan_j(x, nb, t):
    g = nb * N_RWKV_HEADS // SCAN_BH
    bp = nb // g
    x = x.reshape(g, bp, t, N_RWKV_HEADS, SCAN_JHI, SCAN_JLO)
    return jnp.transpose(x, (0, 2, 5, 4, 1, 3)).reshape(g, t, SCAN_JLO, LANES)


def _to_scan_i(x, nb, t):
    g = nb * N_RWKV_HEADS // SCAN_BH
    bp = nb // g
    x = x.reshape(g, bp, t, N_RWKV_HEADS, RWKV_HD)
    x = jnp.transpose(x, (0, 2, 4, 1, 3)).reshape(g, t, RWKV_HD, 1, SCAN_BH)
    return jnp.broadcast_to(x, (g, t, RWKV_HD, SCAN_JHI, SCAN_BH)).reshape(g, t, RWKV_HD, LANES)


def _to_scan_s(x, nb, t):
    g = nb * N_RWKV_HEADS // SCAN_BH
    bp = nb // g
    x = x.reshape(g, bp, t, N_RWKV_HEADS, RWKV_HD)[..., 0]
    x = jnp.transpose(x, (0, 2, 1, 3)).reshape(g, t, 1, 1, SCAN_BH)
    return jnp.broadcast_to(x, (g, t, 1, SCAN_JHI, SCAN_BH)).reshape(g, t, 1, LANES)


def _from_scan_y(y, nb, t):
    g = y.shape[0]
    bp = nb // g
    y = y[..., :SCAN_BH].reshape(g, t, RWKV_HD, bp, N_RWKV_HEADS)
    return jnp.transpose(y, (0, 3, 1, 4, 2)).reshape(nb * t, RWKV_WIDTH)


def _state_to_scan(s, nb):
    g = nb * N_RWKV_HEADS // SCAN_BH
    bp = nb // g
    s = s.reshape(g, bp, N_RWKV_HEADS, RWKV_HD, SCAN_JHI, SCAN_JLO)
    return jnp.transpose(s, (0, 5, 3, 4, 1, 2)).reshape(g, SCAN_JLO, RWKV_HD, LANES)


def _state_from_scan(s, nb):
    g = s.shape[0]
    bp = nb // g
    s = s.reshape(g, SCAN_JLO, RWKV_HD, SCAN_JHI, bp, N_RWKV_HEADS)
    return jnp.transpose(s, (0, 4, 5, 2, 3, 1)).reshape(nb, N_RWKV_HEADS, RWKV_HD, RWKV_HD)


def _mix_body(y_ref, bv_ref, g_ref, od_ref, h_ref, lnxg_ref, lnxb_ref, sub_ref, wout_ref, g1_ref, b1_ref,
              seg_ref, ones_ref, o_ref):
    seg = seg_ref[...]
    y = y_ref[...]
    mu = _seg_sum(y, seg) * (1.0 / RWKV_HD)
    d = y - mu
    var = _seg_sum(d * d, seg) * (1.0 / RWKV_HD)
    yn = d * lax.rsqrt(var + GN_EPS) * lnxg_ref[...] + lnxb_ref[...]
    o_rw = (yn + bv_ref[...]) * g_ref[...]
    od = od_ref[...]
    ms = _seg_sum(od * od, ones_ref[...]) * (1.0 / DIFF_VD)
    o_df = od * lax.rsqrt(ms + RMS_EPS) * sub_ref[...]
    mix = _dot(o_rw.astype(BF16), wout_ref[0:512, :]) + _dot(o_df.astype(BF16), wout_ref[512:1024, :])
    o_ref[...] = _layer_norm(DN_ALPHA * h_ref[...] + mix, g1_ref[...], b1_ref[...])


def mix_ln(y, bv, g, od, h, consts, tm):
    n = y.shape[0]
    row = lambda width: pl.BlockSpec((tm, width), lambda i: (i, 0))
    full = lambda a: pl.BlockSpec(a.shape, lambda i: (0,) * a.ndim)
    return pl.pallas_call(
        _mix_body,
        grid=(n // tm,),
        in_specs=[row(512)] * 4 + [row(D_MODEL)] + [full(c) for c in consts],
        out_specs=row(D_MODEL),
        out_shape=jax.ShapeDtypeStruct((n, D_MODEL), F32),
        compiler_params=_cparams(("arbitrary",)),
        name="mix_ln1",
    )(y, bv, g, od, h, *consts)


def _t5_bucket(dist):
    max_exact = N_BUCKETS // 2
    d = jnp.maximum(dist, 1).astype(F32)
    large = max_exact + (jnp.log(d / max_exact) / math.log(MAX_DISTANCE / max_exact)
                         * (N_BUCKETS - max_exact)).astype(jnp.int32)
    return jnp.where(dist < max_exact, dist, jnp.minimum(large, N_BUCKETS - 1))


def _bias_table(rel_bias, dist):
    bias = rel_bias[_t5_bucket(jnp.maximum(dist, 0))]
    bias = jnp.moveaxis(bias, (-2, -1), (0, 1)).astype(F32)
    return jnp.where(dist >= 0, bias, NEG_INF)


def _dap_body(tq, tk, qi_ref, ki_ref, lam_ref, q_ref, k_ref, v_ref, bias_ref, o_ref,
              qs_scr, m_scr, l_scr, acc_scr):
    step = pl.program_id(2)
    qi = qi_ref[step]
    ki = ki_ref[step]

    @pl.when(ki == 0)
    def _():
        q = q_ref[...] * (DIFF_HD ** -0.5)
        lane = lax.broadcasted_iota(jnp.int32, q.shape, 1)
        qs_scr[0:tq, :] = jnp.where(lane < DIFF_HD, q, 0.0).astype(BF16)
        qs_scr[tq:2 * tq, :] = jnp.where(lane >= DIFF_HD, q, 0.0).astype(BF16)
        m_scr[...] = jnp.full(m_scr.shape, NEG_INF, F32)
        l_scr[...] = jnp.zeros(l_scr.shape, F32)
        acc_scr[...] = jnp.zeros(acc_scr.shape, F32)

    s = _dot_nt(qs_scr[...], k_ref[...].astype(BF16)) + bias_ref[0, 0]
    m_prev = m_scr[...]
    m_new = jnp.maximum(m_prev, jnp.max(s, axis=1, keepdims=True))
    alpha = jnp.exp(m_prev - m_new)
    p = jnp.exp(s - jnp.tile(m_new, (1, tk // LANES)))
    l_scr[...] = alpha * l_scr[...] + jnp.sum(p, axis=1, keepdims=True)
    acc_scr[...] = alpha * acc_scr[...] + _dot(p.astype(BF16), v_ref[...].astype(BF16))
    m_scr[...] = m_new

    @pl.when(ki == qi)
    def _():
        o = acc_scr[...] / l_scr[...]
        o_ref[...] = o[0:tq] - lam_ref[0, 0] * o[tq:2 * tq]


def diff_attn_prompt(q, k, v, rel_bias, lam, nb, t, tq):
    tk = tq
    assert tq >= MAX_DISTANCE and t % tq == 0
    nq = t // tq
    qi_list, ki_list = [], []
    for a in range(nq):
        for c in range(a + 1):
            qi_list.append(a)
            ki_list.append(c)
    qi_arr = jnp.asarray(qi_list, jnp.int32)
    ki_arr = jnp.asarray(ki_list, jnp.int32)
    rr = jnp.arange(tq, dtype=jnp.int32)[:, None]
    cc = jnp.arange(tk, dtype=jnp.int32)[None, :]
    tiles = jnp.stack([_bias_table(rel_bias, jnp.full((tq, tk), 2 * tq, jnp.int32)),
                       _bias_table(rel_bias, tq + rr - cc),
                       _bias_table(rel_bias, rr - cc)], axis=1)
    tiles = tiles.reshape(N_DIFF_HEADS, 3, 2 * tq, tk)

    def kind(qi, ki):
        return jnp.where(ki == qi, 2, jnp.where(ki == qi - 1, 1, 0))

    grid_spec = pltpu.PrefetchScalarGridSpec(
        num_scalar_prefetch=2,
        grid=(nb, N_DIFF_HEADS, len(qi_list)),
        in_specs=[
            pl.BlockSpec(memory_space=pltpu.SMEM),
            pl.BlockSpec((tq, LANES), lambda b, h, s, qi, ki: (b * nq + qi[s], h)),
            pl.BlockSpec((tk, LANES), lambda b, h, s, qi, ki: (b * nq + ki[s], h)),
            pl.BlockSpec((tk, LANES), lambda b, h, s, qi, ki: (b * nq + ki[s], h)),
            pl.BlockSpec((1, 1, 2 * tq, tk), lambda b, h, s, qi, ki: (h, kind(qi[s], ki[s]), 0, 0)),
        ],
        out_specs=pl.BlockSpec((tq, LANES), lambda b, h, s, qi, ki: (b * nq + qi[s], h)),
        scratch_shapes=[pltpu.VMEM((2 * tq, LANES), BF16), pltpu.VMEM((2 * tq, LANES), F32),
                        pltpu.VMEM((2 * tq, LANES), F32), pltpu.VMEM((2 * tq, LANES), F32)],
    )
    return pl.pallas_call(
        functools.partial(_dap_body, tq, tk),
        grid_spec=grid_spec,
        out_shape=jax.ShapeDtypeStruct((nb * t, DIFF_WIDTH), F32),
        compiler_params=_cparams(("arbitrary", "arbitrary", "arbitrary")),
        name="diff_attn_prompt",
    )(qi_arr, ki_arr, lam, q, k, v, tiles)


def _das_body(npg, s_len, pt_ref, lam_ref, q_ref, qmask_ref, bias_ref, bnew_ref, kn_ref, vn_ref, *rest):
    k_refs = rest[0:npg]
    v_refs = rest[npg:2 * npg]
    o_ref = rest[2 * npg]
    qs_scr, m_scr, l_scr, acc_scr = rest[2 * npg + 1:]
    g = pl.program_id(1)
    rows = 2 * N_DIFF_HEADS * s_len

    @pl.when(g == 0)
    def _():
        q = q_ref[0] * (DIFF_HD ** -0.5)
        qrep = jnp.concatenate([q] * (2 * N_DIFF_HEADS), axis=0)
        qs_scr[...] = (qrep * qmask_ref[...]).astype(BF16)
        m_scr[...] = jnp.full(m_scr.shape, NEG_INF, F32)
        l_scr[...] = jnp.zeros(l_scr.shape, F32)
        acc_scr[...] = jnp.zeros(acc_scr.shape, F32)

    qs = qs_scr[...]
    s = jnp.concatenate([_dot_nt(qs, kr[0].astype(BF16)) for kr in k_refs], axis=1) + bias_ref[0]
    m_prev = m_scr[...]
    m_new = jnp.maximum(m_prev, jnp.max(s, axis=1, keepdims=True))
    alpha = jnp.exp(m_prev - m_new)
    p = jnp.exp(s - jnp.tile(m_new, (1, npg))).astype(BF16)
    l_scr[...] = alpha * l_scr[...] + jnp.sum(p.astype(F32), axis=1, keepdims=True)
    pv = _dot(p[:, 0:PAGE_SIZE], v_refs[0][0].astype(BF16))
    for j in range(1, npg):
        pv = pv + _dot(p[:, j * PAGE_SIZE:(j + 1) * PAGE_SIZE], v_refs[j][0].astype(BF16))
    acc_scr[...] = jnp.tile(alpha, (1, DIFF_WIDTH // LANES)) * acc_scr[...] + pv
    m_scr[...] = m_new

    @pl.when(g == pl.num_programs(1) - 1)
    def _():
        qf = qs_scr[...].astype(F32)
        kn = kn_ref[0].astype(BF16).astype(F32)
        vn = vn_ref[0].astype(BF16).astype(F32)
        lane = lax.broadcasted_iota(jnp.int32, (rows, LANES), 1)
        sn = bnew_ref[...]
        for j in range(s_len):
            dj = jnp.sum(qf * kn[j:j + 1, :], axis=1, keepdims=True)
            sn = sn + jnp.where(lane == j, dj, 0.0)
        m_prev = m_scr[...]
        m_new = jnp.maximum(m_prev, jnp.max(sn, axis=1, keepdims=True))
        alpha = jnp.exp(m_prev - m_new)
        pn = jnp.exp(sn - m_new).astype(BF16).astype(F32)
        l_fin = alpha * l_scr[...] + jnp.sum(pn, axis=1, keepdims=True)
        acc = jnp.tile(alpha, (1, DIFF_WIDTH // LANES)) * acc_scr[...]
        for j in range(s_len):
            acc = acc + pn[:, j:j + 1] * vn[j:j + 1, :]
        o = acc / jnp.tile(l_fin, (1, DIFF_WIDTH // LANES))
        lam = lam_ref[0, 0]
        outs = []
        for h in range(N_DIFF_HEADS):
            r0 = h * 2 * s_len
            blk = o[r0:r0 + 2 * s_len, h * DIFF_VD:(h + 1) * DIFF_VD]
            outs.append(blk[0:s_len] - lam * blk[s_len:2 * s_len])
        o_ref[0] = jnp.concatenate(outs, axis=1)


def diff_attn_sample(q, k, v, cache_k, cache_v, page_table, rel_bias, lam, npg):
    db, s_len, _ = q.shape
    n_pages = page_table.shape[1]
    past = n_pages * PAGE_SIZE
    assert n_pages % npg == 0
    ngroups = n_pages // npg
    rows = 2 * N_DIFF_HEADS * s_len
    width = npg * PAGE_SIZE
    hm = jnp.arange(rows, dtype=jnp.int32) // s_len
    qmask = (jnp.arange(DIFF_WIDTH, dtype=jnp.int32)[None, :] // DIFF_HD == hm[:, None]).astype(F32)
    qpos = past + jnp.arange(s_len, dtype=jnp.int32)

    def rows_of(table):
        return table.reshape(rows, table.shape[-1])

    far = rows_of(_bias_table(rel_bias, jnp.full((s_len, width), 2 * MAX_DISTANCE, jnp.int32)))
    kpos_last = past - width + jnp.arange(width, dtype=jnp.int32)
    last = rows_of(_bias_table(rel_bias, qpos[:, None] - kpos_last[None, :]))
    bias = jnp.stack([far, last], axis=0)
    knew_pos = past + jnp.arange(LANES, dtype=jnp.int32)
    dist_new = jnp.where(jnp.arange(LANES)[None, :] < s_len, qpos[:, None] - knew_pos[None, :], -1)
    bnew = rows_of(_bias_table(rel_bias, dist_new))
    assert width >= MAX_DISTANCE + s_len
    pt_flat = page_table.reshape(-1).astype(jnp.int32)

    def page_spec(j):
        return pl.BlockSpec((1, PAGE_SIZE, DIFF_WIDTH),
                            lambda b, g, pt, j=j: (pt[b * n_pages + g * npg + j], 0, 0))

    seq_spec = pl.BlockSpec((1, s_len, DIFF_WIDTH), lambda b, g, pt: (b, 0, 0))
    full2 = lambda a: pl.BlockSpec(a.shape, lambda b, g, pt: (0,) * a.ndim)
    grid_spec = pltpu.PrefetchScalarGridSpec(
        num_scalar_prefetch=1,
        grid=(db, ngroups),
        in_specs=[pl.BlockSpec(memory_space=pltpu.SMEM), seq_spec, full2(qmask),
                  pl.BlockSpec((1, rows, width), lambda b, g, pt: (jnp.where(g == ngroups - 1, 1, 0), 0, 0)),
                  full2(bnew), seq_spec, seq_spec]
                 + [page_spec(j) for j in range(npg)] + [page_spec(j) for j in range(npg)],
        out_specs=seq_spec,
        scratch_shapes=[pltpu.VMEM((rows, DIFF_WIDTH), BF16), pltpu.VMEM((rows, LANES), F32),
                        pltpu.VMEM((rows, LANES), F32), pltpu.VMEM((rows, DIFF_WIDTH), F32)],
    )
    return pl.pallas_call(
        functools.partial(_das_body, npg, s_len),
        grid_spec=grid_spec,
        out_shape=jax.ShapeDtypeStruct((db, s_len, DIFF_WIDTH), F32),
        compiler_params=_cparams(("arbitrary", "arbitrary")),
        name="diff_attn_sample",
    )(pt_flat, lam, q, qmask, bias, bnew, k, v, *([cache_k] * npg), *([cache_v] * npg))


def _mem_attn_body(q_ref, mk_ref, mv_ref, o_ref):
    for h in range(N_MEM_HEADS):
        cols = slice(h * MEM_HD, (h + 1) * MEM_HD)
        qh = (q_ref[0, :, cols] * (MEM_HD ** -0.5)).astype(BF16)
        s = _dot_nt(qh, mk_ref[0, :, cols].astype(BF16))
        p = jnp.exp(s - jnp.max(s, axis=1, keepdims=True))
        l = jnp.sum(p, axis=1, keepdims=True)
        o_ref[0, :, cols] = _dot(p.astype(BF16), mv_ref[0, :, cols].astype(BF16)) / l


def mem_attn(q, mk, mv, tm):
    nb, t, _ = q.shape
    qspec = pl.BlockSpec((1, tm, D_MODEL), lambda b, i: (b, i, 0))
    mspec = pl.BlockSpec((1, N_MEM, D_MODEL), lambda b, i: (b, 0, 0))
    return pl.pallas_call(
        _mem_attn_body,
        grid=(nb, t // tm),
        in_specs=[qspec, mspec, mspec],
        out_specs=qspec,
        out_shape=jax.ShapeDtypeStruct((nb, t, D_MODEL), F32),
        compiler_params=_cparams(("arbitrary", "arbitrary")),
        name="mem_attn",
    )(q, mk, mv)


def _route(x, wr_t, e_bias):
    tm = x.shape[0]
    logits = lax.dot_general(wr_t, x, (((1,), (1,)), ((), ())), precision=lax.Precision.HIGHEST,
                             preferred_element_type=F32)
    scores = jax.nn.sigmoid(logits)
    biased = scores + e_bias
    sub = lax.broadcasted_iota(jnp.int32, (GROUP_SIZE, tm), 0)
    grp_rows = []
    for gi in range(N_GROUPS):
        xg = biased[gi * GROUP_SIZE:(gi + 1) * GROUP_SIZE, :]
        m1 = jnp.max(xg, axis=0, keepdims=True)
        i1 = jnp.min(jnp.where(xg == m1, sub, GROUP_SIZE), axis=0, keepdims=True)
        m2 = jnp.max(jnp.where(sub == i1, -jnp.inf, xg), axis=0, keepdims=True)
        grp_rows.append(m1 + m2)
    grp = jnp.concatenate(grp_rows, axis=0)
    gidx = lax.broadcasted_iota(jnp.int32, (N_GROUPS, tm), 0)
    grank = jnp.zeros((N_GROUPS, tm), jnp.int32)
    for gi in range(N_GROUPS):
        other = jnp.broadcast_to(grp[gi:gi + 1, :], (N_GROUPS, tm))
        grank = grank + ((other > grp) | ((other == grp) & (gi < gidx))).astype(jnp.int32)
    gsel = grank < TOPK_GROUPS
    emask = jnp.concatenate([jnp.broadcast_to(gsel[gi:gi + 1, :], (GROUP_SIZE, tm)) for gi in range(N_GROUPS)],
                            axis=0)
    masked = jnp.where(emask, biased, NEG_INF)
    eidx = lax.broadcasted_iota(jnp.int32, (N_EXPERTS, tm), 0)
    erank = jnp.zeros((N_EXPERTS, tm), jnp.int32)
    for e in range(N_EXPERTS):
        other = jnp.broadcast_to(masked[e:e + 1, :], (N_EXPERTS, tm))
        erank = erank + ((other > masked) | ((other == masked) & (e < eidx))).astype(jnp.int32)
    sel = jnp.where(erank < TOP_K, scores, 0.0)
    gates = sel / jnp.sum(sel, axis=0, keepdims=True) * ROUTED_SCALE
    return jnp.concatenate([gates, jnp.zeros_like(gates)], axis=0)


def _moe_body(x_ref, wrt_ref, eb_ref, wsg_ref, wsu_ref, wsd_ref, wg_ref, wu_ref, wd_ref, g3_ref, b3_ref,
              o_ref, xb_scr, gate_scr, acc_scr):
    e = pl.program_id(1)

    @pl.when(e == 0)
    def _():
        x = x_ref[...]
        xb = x.astype(BF16)
        xb_scr[...] = xb
        gate_scr[...] = jnp.transpose(_route(x, wrt_ref[...], eb_ref[...]))
        hs = jax.nn.silu(_dot(xb, wsg_ref[...])) * _dot(xb, wsu_ref[...])
        acc_scr[...] = _dot(hs.astype(BF16), wsd_ref[...])

    xb = xb_scr[...]
    hg = _dot(xb, wg_ref[0].astype(BF16))
    hu = _dot(xb, wu_ref[0].astype(BF16))
    lane = lax.broadcasted_iota(jnp.int32, gate_scr.shape, 1)
    gcol = jnp.sum(jnp.where(lane == e, gate_scr[...], 0.0), axis=1, keepdims=True)
    act = jax.nn.silu(hg) * hu * gcol
    acc_scr[...] = acc_scr[...] + _dot(act.astype(BF16), wd_ref[0].astype(BF16))

    @pl.when(e == pl.num_programs(1) - 1)
    def _():
        o_ref[...] = _layer_norm(DN_ALPHA * x_ref[...] + acc_scr[...], g3_ref[...], b3_ref[...])


def moe_ln(x, wr_t, e_bias, wsg, wsu, wsd, w_e_gate, w_e_up, w_e_down, g3, b3, tm):
    n = x.shape[0]
    row = pl.BlockSpec((tm, D_MODEL), lambda i, e: (i, 0))
    full = lambda a: pl.BlockSpec(a.shape, lambda i, e: (0,) * a.ndim)
    return pl.pallas_call(
        _moe_body,
        grid=(n // tm, N_EXPERTS),
        in_specs=[row, full(wr_t), full(e_bias), full(wsg), full(wsu), full(wsd),
                  pl.BlockSpec((1, D_MODEL, D_EXPERT), lambda i, e: (e, 0, 0)),
                  pl.BlockSpec((1, D_MODEL, D_EXPERT), lambda i, e: (e, 0, 0)),
                  pl.BlockSpec((1, D_EXPERT, D_MODEL), lambda i, e: (e, 0, 0)),
                  full(g3), full(b3)],
        out_specs=row,
        out_shape=jax.ShapeDtypeStruct((n, D_MODEL), F32),
        scratch_shapes=[pltpu.VMEM((tm, D_MODEL), BF16), pltpu.VMEM((tm, LANES), F32),
                        pltpu.VMEM((tm, D_MODEL), F32)],
        compiler_params=_cparams(("arbitrary", "arbitrary")),
        name="moe_ln3",
    )(x, wr_t, e_bias, wsg, wsu, wsd, w_e_gate, w_e_up, w_e_down, g3, b3)


def _block_ones(seg):
    idx = jnp.arange(LANES, dtype=jnp.int32) // seg
    return (idx[:, None] == idx[None, :]).astype(BF16)


def _row(v):
    return v.reshape(1, -1).astype(F32)


def kernel(x_prompt, x_sample, cache_k, cache_v, state_wkv, state_shift, cache_mem_k, cache_mem_v, page_table, mem_prompt, ln_in_g, ln_in_b, rel_bias, w_in, mu_shift, w0, w_decay_up, a0, w_aaa_up, w_gate_up, k_k, k_a, r_k, lnx_g, lnx_b, lam_q1, lam_k1, lam_q2, lam_k2, subln_g, w_out, ln1_g, ln1_b, w_mq, w_mk, w_mv, w_mo, ln2_g, ln2_b, w_router, e_bias, w_e_gate, w_e_up, w_e_down, w_s_gate, w_s_up, w_s_down, ln3_g, ln3_b):
    assert w_in.shape[0] == DEPTH == 1
    nb, t, _ = x_prompt.shape
    db, s_len, _ = x_sample.shape
    l = 0
    seg64 = _block_ones(RWKV_HD)
    ones128 = _block_ones(LANES)
    zeros = jnp.zeros((64, 512), F32)
    w_lo = jnp.concatenate([jnp.concatenate([w_decay_up[l], zeros], axis=1),
                            jnp.concatenate([zeros, w_aaa_up[l]], axis=1)], axis=0).astype(BF16)
    prep_consts = (_row(mu_shift[l]), w_lo, _row(jnp.concatenate([w0[l], a0[l]])), w_gate_up[l].astype(BF16),
                   _row(k_k[l]), _row(k_a[l]), _row(r_k[l]), seg64)
    mix_consts = (_row(lnx_g[l]), _row(lnx_b[l]), _row(jnp.tile(subln_g[l] * (1.0 - LAM_INIT), N_DIFF_HEADS)),
                  w_out[l].astype(BF16), _row(ln1_g[l]), _row(ln1_b[l]), seg64, ones128)
    lam = (jnp.exp(jnp.sum(lam_q1[l] * lam_k1[l]).astype(F32)) - jnp.exp(jnp.sum(lam_q2[l] * lam_k2[l]).astype(F32))
           + LAM_INIT).reshape(1, 1)
    w_in_b = w_in[l].astype(BF16)
    w_mq_b, w_mk_b, w_mv_b, w_mo_b = (w[l].astype(BF16) for w in (w_mq, w_mk, w_mv, w_mo))
    wr_t = jnp.transpose(w_router[l])
    eb = e_bias[l].reshape(N_EXPERTS, 1).astype(F32)
    wsg, wsu, wsd = w_s_gate[l].astype(BF16), w_s_up[l].astype(BF16), w_s_down[l].astype(BF16)
    g_in, b_in = _row(ln_in_g), _row(ln_in_b)
    g2, b2, g3, b3 = _row(ln2_g[l]), _row(ln2_b[l]), _row(ln3_g[l]), _row(ln3_b[l])

    def rwkv(p, prev, nseq, tlen, s0, tm, shift_in_kernel, tc):
        a, w, wr, b, k2, v, br, kr, g, bv = rwkv_prep(p, prev, prep_consts, tm, shift_in_kernel, tlen)
        y, st = wkv_scan(*[_to_scan_j(z, nseq, tlen) for z in (a, w, wr, b, k2)], _to_scan_i(v, nseq, tlen),
                         _to_scan_s(br, nseq, tlen), _to_scan_s(kr, nseq, tlen), s0, tc)
        return _from_scan_y(y, nseq, tlen), bv, g, _state_from_scan(st, nseq)

    def tail(h1, qm, mk, mv, nseq, tpad, tm, tm_moe):
        n = h1.shape[0]
        om = mem_attn(qm, mk, mv, min(tm, tpad))[:, :n // nseq].reshape(n, D_MODEL)
        h2 = matmul_res_ln(om, h1, w_mo_b, g2, b2, tm, "mem_out_ln2")
        return moe_ln(h2, wr_t, eb, wsg, wsu, wsd, w_e_gate[l], w_e_up[l], w_e_down[l], g3, b3, tm_moe)

    n_p = nb * t
    tm = 512
    hp, pp, qp, kp, vp = ln_proj(x_prompt.reshape(n_p, D_MODEL), g_in, b_in, w_in_b, tm)
    s0_p = jnp.zeros((nb * N_RWKV_HEADS // SCAN_BH, SCAN_JLO, RWKV_HD, LANES), F32)
    y_p, bv_p, gt_p, wkv_p = rwkv(pp, None, nb, t, s0_p, tm, True, 128)
    od_p = diff_attn_prompt(qp, kp, vp, rel_bias, lam, nb, t, 512)
    h1_p = mix_ln(y_p, bv_p, gt_p, od_p, hp, mix_consts, tm)
    mem_rows = mem_prompt.reshape(nb * N_MEM, D_MODEL)
    mk_p = matmul(mem_rows, w_mk_b, tm, "mem_k")
    mv_p = matmul(mem_rows, w_mv_b, tm, "mem_v")
    qm_p = matmul(h1_p, w_mq_b, tm, "mem_q").reshape(nb, t, D_MODEL)
    h3_p = tail(h1_p, qm_p, mk_p.reshape(nb, N_MEM, D_MODEL), mv_p.reshape(nb, N_MEM, D_MODEL), nb, t, tm, 1024)

    n_s = db * s_len
    hs, ps, qs, ks, vs = ln_proj(x_sample.reshape(n_s, D_MODEL), g_in, b_in, w_in_b, n_s)
    ps3 = ps.reshape(db, s_len, RWKV_COLS)
    prev_s = jnp.concatenate([state_shift[l][:, None, :], ps3[:, :-1]], axis=1).reshape(n_s, RWKV_COLS)
    y_s, bv_s, gt_s, wkv_s = rwkv(ps, prev_s, db, s_len, _state_to_scan(state_wkv[l], db), n_s, False, s_len)
    n_pool = cache_k.shape[1]
    od_s = diff_attn_sample(qs.reshape(db, s_len, 512), ks.reshape(db, s_len, 512), vs.reshape(db, s_len, 512),
                            cache_k[l].reshape(n_pool, PAGE_SIZE, DIFF_WIDTH),
                            cache_v[l].reshape(n_pool, PAGE_SIZE, DIFF_WIDTH), page_table, rel_bias, lam, 16)
    h1_s = mix_ln(y_s, bv_s, gt_s, od_s.reshape(n_s, DIFF_WIDTH), hs, mix_consts, n_s)
    qm_s = matmul(h1_s, w_mq_b, n_s, "mem_q").reshape(db, s_len, D_MODEL)
    qm_s = jnp.pad(qm_s, ((0, 0), (0, SUBLANES - s_len), (0, 0)))
    h3_s = tail(h1_s, qm_s, cache_mem_k[l].reshape(db, N_MEM, D_MODEL), cache_mem_v[l].reshape(db, N_MEM, D_MODEL),
                db, SUBLANES, n_s, n_s)

    return (h3_p.reshape(nb, t, D_MODEL), h3_s.reshape(db, s_len, D_MODEL),
            kp.reshape(1, nb, t, N_DIFF_HEADS, 2, DIFF_HD), vp.reshape(1, nb, t, N_DIFF_HEADS, DIFF_VD),
            wkv_p[None], pp.reshape(nb, t, RWKV_COLS)[:, -1][None],
            mk_p.reshape(1, nb, N_MEM, N_MEM_HEADS, MEM_HD), mv_p.reshape(1, nb, N_MEM, N_MEM_HEADS, MEM_HD),
            ks.reshape(1, db, s_len, N_DIFF_HEADS, 2, DIFF_HD), vs.reshape(1, db, s_len, N_DIFF_HEADS, DIFF_VD),
            wkv_s[None], ps3[:, -1][None])
```

```python
import functools
import math

import jax
import jax.numpy as jnp
from jax import lax
from jax.experimental import pallas as pl
from jax.experimental.pallas import tpu as pltpu

F32 = jnp.float32
BF16 = jnp.bfloat16

D_MODEL = 1024
RWKV_WIDTH = 512
RWKV_HD = 64
N_RWKV_HEADS = 8
RWKV_COLS = 1792
GN_EPS = 64e-5
DIFF_WIDTH = 512
DIFF_HD = 64
DIFF_VD = 128
N_DIFF_HEADS = 4
RMS_EPS = 1e-5
N_BUCKETS = 32
MAX_DISTANCE = 128
PAGE_SIZE = 128
N_MEM = 256
N_MEM_HEADS = 4
MEM_HD = 256
N_EXPERTS = 64
N_GROUPS = 8
GROUP_SIZE = 8
TOPK_GROUPS = 4
TOP_K = 8
D_EXPERT = 256
ROUTED_SCALE = 2.5
DEPTH = 1
DN_ALPHA = (2.0 * DEPTH) ** 0.25
LN_EPS = 1e-5
NEG_INF = -1e30
LAM_INIT = 0.8 - 0.6 * math.exp(-0.3 * 0)

LANES = 128
SUBLANES = 8
VMEM_LIMIT = 56 * 1024 * 1024

SCAN_JLO = 16
SCAN_JHI = RWKV_HD // SCAN_JLO
SCAN_BH = LANES // SCAN_JHI


def _cparams(sem):
    return pltpu.CompilerParams(dimension_semantics=sem, vmem_limit_bytes=VMEM_LIMIT)


def _layer_norm(x, g, b):
    mu = jnp.mean(x, -1, keepdims=True)
    d = x - mu
    var = jnp.mean(d * d, -1, keepdims=True)
    return d * lax.rsqrt(var + LN_EPS) * g + b


def _dot(a, b):
    return jnp.dot(a, b, preferred_element_type=F32)


def _dot_nt(a, b):
    return lax.dot_general(a, b, (((1,), (1,)), ((), ())), preferred_element_type=F32)


def _seg_sum(x, ones_blk):
    outs = []
    for c in range(x.shape[1] // LANES):
        xc = x[:, c * LANES:(c + 1) * LANES]
        hi = xc.astype(BF16)
        lo = (xc - hi.astype(F32)).astype(BF16)
        outs.append(_dot(hi, ones_blk) + _dot(lo, ones_blk))
    return outs[0] if len(outs) == 1 else jnp.concatenate(outs, axis=1)


def _ln_proj_body(x_ref, g_ref, b_ref, w_ref, h_ref, p_ref, q_ref, k_ref, v_ref):
    h = _layer_norm(x_ref[...], g_ref[...], b_ref[...])
    h_ref[...] = h
    hb = h.astype(BF16)
    c0 = RWKV_COLS
    p_ref[...] = _dot(hb, w_ref[:, 0:c0])
    q_ref[...] = _dot(hb, w_ref[:, c0:c0 + 512])
    k_ref[...] = _dot(hb, w_ref[:, c0 + 512:c0 + 1024])
    v_ref[...] = _dot(hb, w_ref[:, c0 + 1024:c0 + 1536])


def ln_proj(x, g, b, w_bf16, tm):
    n = x.shape[0]
    row = lambda width: pl.BlockSpec((tm, width), lambda i: (i, 0))
    full = lambda a: pl.BlockSpec(a.shape, lambda i: (0,) * a.ndim)
    return pl.pallas_call(
        _ln_proj_body,
        grid=(n // tm,),
        in_specs=[row(D_MODEL), full(g), full(b), full(w_bf16)],
        out_specs=[row(D_MODEL), row(RWKV_COLS), row(512), row(512), row(512)],
        out_shape=[jax.ShapeDtypeStruct((n, wd), F32) for wd in (D_MODEL, RWKV_COLS, 512, 512, 512)],
        compiler_params=_cparams(("arbitrary",)),
        name="ln_proj",
    )(x, g, b, w_bf16)


def _mm_body(x_ref, w_ref, o_ref):
    o_ref[...] = _dot(x_ref[...].astype(BF16), w_ref[...])


def matmul(x, w_bf16, tm, name):
    n, kdim = x.shape
    nout = w_bf16.shape[1]
    return pl.pallas_call(
        _mm_body,
        grid=(n // tm,),
        in_specs=[pl.BlockSpec((tm, kdim), lambda i: (i, 0)), pl.BlockSpec((kdim, nout), lambda i: (0, 0))],
        out_specs=pl.BlockSpec((tm, nout), lambda i: (i, 0)),
        out_shape=jax.ShapeDtypeStruct((n, nout), F32),
        compiler_params=_cparams(("arbitrary",)),
        name=name,
    )(x, w_bf16)


def _mm_res_ln_body(x_ref, res_ref, w_ref, g_ref, b_ref, o_ref):
    y = _dot(x_ref[...].astype(BF16), w_ref[...])
    o_ref[...] = _layer_norm(DN_ALPHA * res_ref[...] + y, g_ref[...], b_ref[...])


def matmul_res_ln(x, res, w_bf16, g, b, tm, name):
    n, kdim = x.shape
    nout = w_bf16.shape[1]
    row = lambda width: pl.BlockSpec((tm, width), lambda i: (i, 0))
    full = lambda a: pl.BlockSpec(a.shape, lambda i: (0,) * a.ndim)
    return pl.pallas_call(
        _mm_res_ln_body,
        grid=(n // tm,),
        in_specs=[row(kdim), row(nout), full(w_bf16), full(g), full(b)],
        out_specs=row(nout),
        out_shape=jax.ShapeDtypeStruct((n, nout), F32),
        compiler_params=_cparams(("arbitrary",)),
        name=name,
    )(x, res, w_bf16, g, b)


def _rwkv_prep_body(shift_in_kernel, seq_blocks, p_ref, prev_ref, mu_ref, wlo_ref, w0a0_ref, wg_ref,
                    kk_ref, ka_ref, rk_ref, seg_ref,
                    a_o, w_o, wr_o, b_o, k_o, v_o, br_o, kr_o, g_o, bv_o):
    p = p_ref[...]
    if shift_in_kernel:
        first = jnp.where(pl.program_id(0) % seq_blocks == 0, 0.0, prev_ref[SUBLANES - 1:SUBLANES, :])
        row = lax.broadcasted_iota(jnp.int32, p.shape, 0)
        p_prev = jnp.where(row == 0, first, pltpu.roll(p, 1, 0))
    else:
        p_prev = prev_ref[...]
    ps = p + (p_prev - p) * mu_ref[...]
    r = ps[:, 0:512]
    k = ps[:, 512:1024]
    v = ps[:, 1024:1536]
    lo = ps[:, 1536:1664]
    g_lo = ps[:, 1664:1792]
    lane = lax.broadcasted_iota(jnp.int32, lo.shape, 1)
    lo = jnp.where(lane < 64, jnp.tanh(lo), lo)
    wa = _dot(lo.astype(BF16), wlo_ref[...]) + w0a0_ref[...]
    x = -wa[:, 0:512]
    softplus = jnp.maximum(x, 0.0) + jnp.log(1.0 + jnp.exp(-jnp.abs(x)))
    decay = jnp.exp(-jnp.exp(-softplus - 0.5))
    a = jax.nn.sigmoid(wa[:, 512:1024])
    g = _dot(jax.nn.sigmoid(g_lo).astype(BF16), wg_ref[...])
    seg = seg_ref[...]
    kk = k * kk_ref[...]
    kk = kk / jnp.maximum(jnp.sqrt(_seg_sum(kk * kk, seg)), 1e-12)
    k2 = k * (1.0 + (a - 1.0) * ka_ref[...])
    b = kk * a
    a_o[...] = -kk
    w_o[...] = decay
    wr_o[...] = decay * r
    b_o[...] = b
    k_o[...] = k2
    v_o[...] = v
    br_o[...] = _seg_sum(b * r, seg)
    kr_o[...] = _seg_sum(k2 * r, seg)
    g_o[...] = g
    bv_o[...] = _seg_sum(r * k2 * rk_ref[...], seg) * v


def rwkv_prep(p, prev, consts, tm, shift_in_kernel, seq_len):
    n = p.shape[0]
    row = lambda width: pl.BlockSpec((tm, width), lambda i: (i, 0))
    full = lambda a: pl.BlockSpec(a.shape, lambda i: (0,) * a.ndim)
    if shift_in_kernel:
        per8 = tm // SUBLANES
        prev_spec = pl.BlockSpec((SUBLANES, RWKV_COLS), lambda i: (jnp.maximum(i * per8 - 1, 0), 0))
        prev = p
    else:
        prev_spec = row(RWKV_COLS)
    body = functools.partial(_rwkv_prep_body, shift_in_kernel, max(seq_len // tm, 1))
    return pl.pallas_call(
        body,
        grid=(n // tm,),
        in_specs=[row(RWKV_COLS), prev_spec] + [full(c) for c in consts],
        out_specs=[row(512)] * 10,
        out_shape=[jax.ShapeDtypeStruct((n, 512), F32)] * 10,
        compiler_params=_cparams(("arbitrary",)),
        name="rwkv_prep",
    )(p, prev, *consts)


def _lane_fold(x):
    return x + pltpu.roll(x, 32, 1) + pltpu.roll(x, 64, 1) + pltpu.roll(x, 96, 1)


def _wkv_scan_body(tc, a_ref, w_ref, wr_ref, b_ref, k_ref, v_ref, br_ref, kr_ref, s0_ref,
                   y_ref, st_ref, s_scr):
    c = pl.program_id(1)
    ngrp = RWKV_HD // SUBLANES

    @pl.when(c == 0)
    def _():
        s_scr[...] = s0_ref[0]

    def bcast(ref, t, j):
        return jnp.broadcast_to(ref[0, t, pl.ds(j, 1), :], (SUBLANES, LANES))

    def step(t, carry):
        acc_a = [jnp.zeros((SUBLANES, LANES), F32) for _ in range(ngrp)]
        acc_y = [jnp.zeros((SUBLANES, LANES), F32) for _ in range(ngrp)]
        for j in range(SCAN_JLO):
            ab = bcast(a_ref, t, j)
            wrb = bcast(wr_ref, t, j)
            for ig in range(ngrp):
                s = s_scr[j, ig * SUBLANES:(ig + 1) * SUBLANES, :]
                acc_a[ig] = acc_a[ig] + s * ab
                acc_y[ig] = acc_y[ig] + s * wrb
        sa = [_lane_fold(x) for x in acc_a]
        vv = [v_ref[0, t, ig * SUBLANES:(ig + 1) * SUBLANES, :] for ig in range(ngrp)]
        for j in range(SCAN_JLO):
            wb = bcast(w_ref, t, j)
            bb = bcast(b_ref, t, j)
            kb = bcast(k_ref, t, j)
            for ig in range(ngrp):
                sl = slice(ig * SUBLANES, (ig + 1) * SUBLANES)
                s_scr[j, sl, :] = s_scr[j, sl, :] * wb + sa[ig] * bb + vv[ig] * kb
        br = jnp.broadcast_to(br_ref[0, t], (SUBLANES, LANES))
        kr = jnp.broadcast_to(kr_ref[0, t], (SUBLANES, LANES))
        for ig in range(ngrp):
            y_ref[0, t, ig * SUBLANES:(ig + 1) * SUBLANES, :] = _lane_fold(acc_y[ig]) + sa[ig] * br + vv[ig] * kr
        return carry

    lax.fori_loop(0, tc, step, 0)

    @pl.when(c == pl.num_programs(1) - 1)
    def _():
        st_ref[0] = s_scr[...]


def wkv_scan(a, w, wr, b, k, v, br, kr, s0, tc):
    g, t = a.shape[0], a.shape[1]
    jspec = pl.BlockSpec((1, tc, SCAN_JLO, LANES), lambda gi, c: (gi, c, 0, 0))
    ispec = pl.BlockSpec((1, tc, RWKV_HD, LANES), lambda gi, c: (gi, c, 0, 0))
    sspec = pl.BlockSpec((1, tc, 1, LANES), lambda gi, c: (gi, c, 0, 0))
    stspec = pl.BlockSpec((1, SCAN_JLO, RWKV_HD, LANES), lambda gi, c: (gi, 0, 0, 0))
    return pl.pallas_call(
        functools.partial(_wkv_scan_body, tc),
        grid=(g, t // tc),
        in_specs=[jspec] * 5 + [ispec, sspec, sspec, stspec],
        out_specs=[ispec, stspec],
        out_shape=[jax.ShapeDtypeStruct((g, t, RWKV_HD, LANES), F32),
                   jax.ShapeDtypeStruct((g, SCAN_JLO, RWKV_HD, LANES), F32)],
        scratch_shapes=[pltpu.VMEM((SCAN_JLO, RWKV_HD, LANES), F32)],
        compiler_params=_cparams(("arbitrary", "arbitrary")),
        name="wkv_scan",
    )(a, w, wr, b, k, v, br, kr, s0)


def _to_scan_j(x, nb, t):
    g = nb * N_RWKV_HEADS // SCAN_BH
    bp = nb // g
    x = x.reshape(g, bp, t, N_RWKV_HEADS, SCAN_JHI, SCAN_JLO)
    return jnp.transpose(x, (0, 2, 5, 4, 1, 3)).reshape(g, t, SCAN_JLO, LANES)


def _to_scan_i(x, nb, t):
    g = nb * N_RWKV_HEADS // SCAN_BH
    bp = nb // g
    x = x.reshape(g, bp, t, N_RWKV_HEADS, RWKV_HD)
    x = jnp.transpose(x, (0, 2, 4, 1, 3)).reshape(g, t, RWKV_HD, 1, SCAN_BH)
    return jnp.broadcast_to(x, (g, t, RWKV_HD, SCAN_JHI, SCAN_BH)).reshape(g, t, RWKV_HD, LANES)


def _to_scan_s(x, nb, t):
    g = nb * N_RWKV_HEADS // SCAN_BH
    bp = nb // g
    x = x.reshape(g, bp, t, N_RWKV_HEADS, RWKV_HD)[..., 0]
    x = jnp.transpose(x, (0, 2, 1, 3)).reshape(g, t, 1, 1, SCAN_BH)
    return jnp.broadcast_to(x, (g, t, 1, SCAN_JHI, SCAN_BH)).reshape(g, t, 1, LANES)


def _from_scan_y(y, nb, t):
    g = y.shape[0]
    bp = nb // g
    y = y[..., :SCAN_BH].reshape(g, t, RWKV_HD, bp, N_RWKV_HEADS)
    return jnp.transpose(y, (0, 3, 1, 4, 2)).reshape(nb * t, RWKV_WIDTH)


def _state_to_scan(s, nb):
    g = nb * N_RWKV_HEADS // SCAN_BH
    bp = nb // g
    s = s.reshape(g, bp, N_RWKV_HEADS, RWKV_HD, SCAN_JHI, SCAN_JLO)
    return jnp.transpose(s, (0, 5, 3, 4, 1, 2)).reshape(g, SCAN_JLO, RWKV_HD, LANES)


def _state_from_scan(s, nb):
    g = s.shape[0]
    bp = nb // g
    s = s.reshape(g, SCAN_JLO, RWKV_HD, SCAN_JHI, bp, N_RWKV_HEADS)
    return jnp.transpose(s, (0, 4, 5, 2, 3, 1)).reshape(nb, N_RWKV_HEADS, RWKV_HD, RWKV_HD)


def _mix_body(y_ref, bv_ref, g_ref, od_ref, h_ref, lnxg_ref, lnxb_ref, sub_ref, wout_ref, g1_ref, b1_ref,
              seg_ref, ones_ref, o_ref):
    seg = seg_ref[...]
    y = y_ref[...]
    mu = _seg_sum(y, seg) * (1.0 / RWKV_HD)
    d = y - mu
    var = _seg_sum(d * d, seg) * (1.0 / RWKV_HD)
    yn = d * lax.rsqrt(var + GN_EPS) * lnxg_ref[...] + lnxb_ref[...]
    o_rw = (yn + bv_ref[...]) * g_ref[...]
    od = od_ref[...]
    ms = _seg_sum(od * od, ones_ref[...]) * (1.0 / DIFF_VD)
    o_df = od * lax.rsqrt(ms + RMS_EPS) * sub_ref[...]
    mix = _dot(o_rw.astype(BF16), wout_ref[0:512, :]) + _dot(o_df.astype(BF16), wout_ref[512:1024, :])
    o_ref[...] = _layer_norm(DN_ALPHA * h_ref[...] + mix, g1_ref[...], b1_ref[...])


def mix_ln(y, bv, g, od, h, consts, tm):
    n = y.shape[0]
    row = lambda width: pl.BlockSpec((tm, width), lambda i: (i, 0))
    full = lambda a: pl.BlockSpec(a.shape, lambda i: (0,) * a.ndim)
    return pl.pallas_call(
        _mix_body,
        grid=(n // tm,),
        in_specs=[row(512)] * 4 + [row(D_MODEL)] + [full(c) for c in consts],
        out_specs=row(D_MODEL),
        out_shape=jax.ShapeDtypeStruct((n, D_MODEL), F32),
        compiler_params=_cparams(("arbitrary",)),
        name="mix_ln1",
    )(y, bv, g, od, h, *consts)


def _t5_bucket(dist):
    max_exact = N_BUCKETS // 2
    d = jnp.maximum(dist, 1).astype(F32)
    large = max_exact + (jnp.log(d / max_exact) / math.log(MAX_DISTANCE / max_exact)
                         * (N_BUCKETS - max_exact)).astype(jnp.int32)
    return jnp.where(dist < max_exact, dist, jnp.minimum(large, N_BUCKETS - 1))


def _bias_table(rel_bias, dist):
    bucket = _t5_bucket(jnp.maximum(dist, 0))
    tail = (1,) * dist.ndim
    bias = jnp.zeros(rel_bias.shape[1:] + dist.shape, F32)
    for bk in range(N_BUCKETS):
        bias = jnp.where(bucket == bk, rel_bias[bk].astype(F32).reshape(rel_bias.shape[1:] + tail), bias)
    return jnp.where(dist >= 0, bias, NEG_INF)


def _dap_body(tq, tk, qi_ref, ki_ref, lam_ref, q_ref, k_ref, v_ref, bias_ref, o_ref,
              qs_scr, m_scr, l_scr, acc_scr):
    step = pl.program_id(2)
    qi = qi_ref[step]
    ki = ki_ref[step]

    @pl.when(ki == 0)
    def _():
        q = q_ref[...] * (DIFF_HD ** -0.5)
        lane = lax.broadcasted_iota(jnp.int32, q.shape, 1)
        qs_scr[0:tq, :] = jnp.where(lane < DIFF_HD, q, 0.0).astype(BF16)
        qs_scr[tq:2 * tq, :] = jnp.where(lane >= DIFF_HD, q, 0.0).astype(BF16)
        m_scr[...] = jnp.full(m_scr.shape, NEG_INF, F32)
        l_scr[...] = jnp.zeros(l_scr.shape, F32)
        acc_scr[...] = jnp.zeros(acc_scr.shape, F32)

    s = _dot_nt(qs_scr[...], k_ref[...].astype(BF16)) + bias_ref[0, 0]
    m_prev = m_scr[...]
    m_new = jnp.maximum(m_prev, jnp.max(s, axis=1, keepdims=True))
    alpha = jnp.exp(m_prev - m_new)
    p = jnp.exp(s - jnp.tile(m_new, (1, tk // LANES)))
    l_scr[...] = alpha * l_scr[...] + jnp.sum(p, axis=1, keepdims=True)
    acc_scr[...] = alpha * acc_scr[...] + _dot(p.astype(BF16), v_ref[...].astype(BF16))
    m_scr[...] = m_new

    @pl.when(ki == qi)
    def _():
        o = acc_scr[...] / l_scr[...]
        o_ref[...] = o[0:tq] - lam_ref[0, 0] * o[tq:2 * tq]


def diff_attn_prompt(q, k, v, rel_bias, lam, nb, t, tq):
    tk = tq
    assert tq >= MAX_DISTANCE and t % tq == 0
    nq = t // tq
    qi_list, ki_list = [], []
    for a in range(nq):
        for c in range(a + 1):
            qi_list.append(a)
            ki_list.append(c)
    qi_arr = jnp.asarray(qi_list, jnp.int32)
    ki_arr = jnp.asarray(ki_list, jnp.int32)
    rr = jnp.arange(tq, dtype=jnp.int32)[:, None]
    cc = jnp.arange(tk, dtype=jnp.int32)[None, :]
    tiles = jnp.stack([_bias_table(rel_bias, jnp.full((tq, tk), 2 * tq, jnp.int32)),
                       _bias_table(rel_bias, tq + rr - cc),
                       _bias_table(rel_bias, rr - cc)], axis=1)
    tiles = tiles.reshape(N_DIFF_HEADS, 3, 2 * tq, tk)

    def kind(qi, ki):
        return jnp.where(ki == qi, 2, jnp.where(ki == qi - 1, 1, 0))

    grid_spec = pltpu.PrefetchScalarGridSpec(
        num_scalar_prefetch=2,
        grid=(nb, N_DIFF_HEADS, len(qi_list)),
        in_specs=[
            pl.BlockSpec(memory_space=pltpu.SMEM),
            pl.BlockSpec((tq, LANES), lambda b, h, s, qi, ki: (b * nq + qi[s], h)),
            pl.BlockSpec((tk, LANES), lambda b, h, s, qi, ki: (b * nq + ki[s], h)),
            pl.BlockSpec((tk, LANES), lambda b, h, s, qi, ki: (b * nq + ki[s], h)),
            pl.BlockSpec((1, 1, 2 * tq, tk), lambda b, h, s, qi, ki: (h, kind(qi[s], ki[s]), 0, 0)),
        ],
        out_specs=pl.BlockSpec((tq, LANES), lambda b, h, s, qi, ki: (b * nq + qi[s], h)),
        scratch_shapes=[pltpu.VMEM((2 * tq, LANES), BF16), pltpu.VMEM((2 * tq, LANES), F32),
                        pltpu.VMEM((2 * tq, LANES), F32), pltpu.VMEM((2 * tq, LANES), F32)],
    )
    return pl.pallas_call(
        functools.partial(_dap_body, tq, tk),
        grid_spec=grid_spec,
        out_shape=jax.ShapeDtypeStruct((nb * t, DIFF_WIDTH), F32),
        compiler_params=_cparams(("arbitrary", "arbitrary", "arbitrary")),
        name="diff_attn_prompt",
    )(qi_arr, ki_arr, lam, q, k, v, tiles)


def _das_body(npg, s_len, pt_ref, lam_ref, q_ref, qmask_ref, bias_ref, bnew_ref, kn_ref, vn_ref, *rest):
    k_refs = rest[0:npg]
    v_refs = rest[npg:2 * npg]
    o_ref = rest[2 * npg]
    qs_scr, m_scr, l_scr, acc_scr = rest[2 * npg + 1:]
    g = pl.program_id(1)
    rows = 2 * N_DIFF_HEADS * s_len

    @pl.when(g == 0)
    def _():
        q = q_ref[0] * (DIFF_HD ** -0.5)
        qrep = jnp.concatenate([q] * (2 * N_DIFF_HEADS), axis=0)
        qs_scr[...] = (qrep * qmask_ref[...]).astype(BF16)
        m_scr[...] = jnp.full(m_scr.shape, NEG_INF, F32)
        l_scr[...] = jnp.zeros(l_scr.shape, F32)
        acc_scr[...] = jnp.zeros(acc_scr.shape, F32)

    hrows = 2 * s_len
    qs = qs_scr[...]
    s = jnp.concatenate([_dot(qs, kr[0].astype(BF16)) for kr in k_refs], axis=1) + bias_ref[0]
    m_prev = m_scr[...]
    m_new = jnp.maximum(m_prev, jnp.max(s, axis=1, keepdims=True))
    alpha = jnp.exp(m_prev - m_new)
    p = jnp.exp(s - jnp.tile(m_new, (1, npg)))
    l_scr[...] = alpha * l_scr[...] + jnp.sum(p, axis=1, keepdims=True)
    pvs = []
    for h in range(N_DIFF_HEADS):
        pv = None
        for j in range(npg):
            ph = p[h * hrows:(h + 1) * hrows, j * PAGE_SIZE:(j + 1) * PAGE_SIZE].astype(BF16)
            vh = v_refs[j][0, pl.ds(h, PAGE_SIZE, stride=N_DIFF_HEADS), :].astype(BF16)
            d = _dot(ph, vh)
            pv = d if pv is None else pv + d
        pvs.append(pv)
    acc_scr[...] = alpha * acc_scr[...] + jnp.concatenate(pvs, axis=0)
    m_scr[...] = m_new

    @pl.when(g == pl.num_programs(1) - 1)
    def _():
        qf = qs_scr[...].astype(F32)
        kn = kn_ref[0].astype(BF16).astype(F32)
        vn = vn_ref[0].astype(BF16).astype(F32)
        lane = lax.broadcasted_iota(jnp.int32, (rows, LANES), 1)
        sn = bnew_ref[...]
        for j in range(s_len):
            dj = jnp.sum(qf * kn[j:j + 1, :], axis=1, keepdims=True)
            sn = sn + jnp.where(lane == j, dj, 0.0)
        m_prev = m_scr[...]
        m_new = jnp.maximum(m_prev, jnp.max(sn, axis=1, keepdims=True))
        alpha = jnp.exp(m_prev - m_new)
        pn = jnp.exp(sn - m_new).astype(BF16).astype(F32)
        l_fin = alpha * l_scr[...] + jnp.sum(pn, axis=1, keepdims=True)
        acc = alpha * acc_scr[...]
        for j in range(s_len):
            vj = jnp.concatenate([jnp.broadcast_to(vn[j:j + 1, h * DIFF_VD:(h + 1) * DIFF_VD], (hrows, DIFF_VD))
                                  for h in range(N_DIFF_HEADS)], axis=0)
            acc = acc + pn[:, j:j + 1] * vj
        o = acc / l_fin
        lam = lam_ref[0, 0]
        outs = []
        for h in range(N_DIFF_HEADS):
            blk = o[h * hrows:(h + 1) * hrows, :]
            outs.append(blk[0:s_len] - lam * blk[s_len:hrows])
        o_ref[0] = jnp.concatenate(outs, axis=1)


def diff_attn_sample(q, k, v, cache_k, cache_v, page_table, rel_bias, lam, npg):
    db, s_len, _ = q.shape
    n_pages = page_table.shape[1]
    past = n_pages * PAGE_SIZE
    assert n_pages % npg == 0
    ngroups = n_pages // npg
    rows = 2 * N_DIFF_HEADS * s_len
    width = npg * PAGE_SIZE
    hm = jnp.arange(rows, dtype=jnp.int32) // s_len
    qmask = (jnp.arange(DIFF_WIDTH, dtype=jnp.int32)[None, :] // DIFF_HD == hm[:, None]).astype(F32)
    qpos = past + jnp.arange(s_len, dtype=jnp.int32)

    def rows_of(table):
        return table.reshape(rows, table.shape[-1])

    far = rows_of(_bias_table(rel_bias, jnp.full((s_len, width), 2 * MAX_DISTANCE, jnp.int32)))
    kpos_last = past - width + jnp.arange(width, dtype=jnp.int32)
    last = rows_of(_bias_table(rel_bias, qpos[:, None] - kpos_last[None, :]))
    bias = jnp.stack([far, last], axis=0)
    knew_pos = past + jnp.arange(LANES, dtype=jnp.int32)
    dist_new = jnp.where(jnp.arange(LANES)[None, :] < s_len, qpos[:, None] - knew_pos[None, :], -1)
    bnew = rows_of(_bias_table(rel_bias, dist_new))
    assert width >= MAX_DISTANCE + s_len
    pt_flat = page_table.reshape(-1).astype(jnp.int32)

    def page_spec(j):
        return pl.BlockSpec((1, DIFF_WIDTH, PAGE_SIZE),
                            lambda b, g, pt, j=j: (pt[b * n_pages + g * npg + j], 0, 0))

    seq_spec = pl.BlockSpec((1, s_len, DIFF_WIDTH), lambda b, g, pt: (b, 0, 0))
    full2 = lambda a: pl.BlockSpec(a.shape, lambda b, g, pt: (0,) * a.ndim)
    grid_spec = pltpu.PrefetchScalarGridSpec(
        num_scalar_prefetch=1,
        grid=(db, ngroups),
        in_specs=[pl.BlockSpec(memory_space=pltpu.SMEM), seq_spec, full2(qmask),
                  pl.BlockSpec((1, rows, width), lambda b, g, pt: (jnp.where(g == ngroups - 1, 1, 0), 0, 0)),
                  full2(bnew), seq_spec, seq_spec]
                 + [page_spec(j) for j in range(npg)] + [page_spec(j) for j in range(npg)],
        out_specs=seq_spec,
        scratch_shapes=[pltpu.VMEM((rows, DIFF_WIDTH), BF16), pltpu.VMEM((rows, LANES), F32),
                        pltpu.VMEM((rows, LANES), F32), pltpu.VMEM((rows, DIFF_VD), F32)],
    )
    return pl.pallas_call(
        functools.partial(_das_body, npg, s_len),
        grid_spec=grid_spec,
        out_shape=jax.ShapeDtypeStruct((db, s_len, DIFF_WIDTH), F32),
        compiler_params=_cparams(("arbitrary", "arbitrary")),
        name="diff_attn_sample",
    )(pt_flat, lam, q, qmask, bias, bnew, k, v, *([cache_k] * npg), *([cache_v] * npg))


def _mem_attn_body(q_ref, mk_ref, mv_ref, o_ref):
    for h in range(N_MEM_HEADS):
        cols = slice(h * MEM_HD, (h + 1) * MEM_HD)
        qh = (q_ref[0, :, cols] * (MEM_HD ** -0.5)).astype(BF16)
        s = _dot_nt(qh, mk_ref[0, :, h, :].astype(BF16))
        p = jnp.exp(s - jnp.max(s, axis=1, keepdims=True))
        l = jnp.sum(p, axis=1, keepdims=True)
        o_ref[0, :, cols] = _dot(p.astype(BF16), mv_ref[0, :, h, :].astype(BF16)) / l


def mem_attn(q, mk, mv, tm):
    nb, t, _ = q.shape
    qspec = pl.BlockSpec((1, tm, D_MODEL), lambda b, i: (b, i, 0))
    mspec = pl.BlockSpec((1, N_MEM, N_MEM_HEADS, MEM_HD), lambda b, i: (b, 0, 0, 0))
    return pl.pallas_call(
        _mem_attn_body,
        grid=(nb, t // tm),
        in_specs=[qspec, mspec, mspec],
        out_specs=qspec,
        out_shape=jax.ShapeDtypeStruct((nb, t, D_MODEL), F32),
        compiler_params=_cparams(("arbitrary", "arbitrary")),
        name="mem_attn",
    )(q, mk, mv)


def _route(x, wr_t, e_bias):
    tm = x.shape[0]
    logits = lax.dot_general(wr_t, x, (((1,), (1,)), ((), ())), precision=lax.Precision.HIGHEST,
                             preferred_element_type=F32)
    scores = jax.nn.sigmoid(logits)
    biased = scores + e_bias
    sub = lax.broadcasted_iota(jnp.int32, (GROUP_SIZE, tm), 0)
    grp_rows = []
    for gi in range(N_GROUPS):
        xg = biased[gi * GROUP_SIZE:(gi + 1) * GROUP_SIZE, :]
        m1 = jnp.max(xg, axis=0, keepdims=True)
        i1 = jnp.min(jnp.where(xg == m1, sub, GROUP_SIZE), axis=0, keepdims=True)
        m2 = jnp.max(jnp.where(sub == i1, -jnp.inf, xg), axis=0, keepdims=True)
        grp_rows.append(m1 + m2)
    grp = jnp.concatenate(grp_rows, axis=0)
    gidx = lax.broadcasted_iota(jnp.int32, (N_GROUPS, tm), 0)
    grank = jnp.zeros((N_GROUPS, tm), jnp.int32)
    for gi in range(N_GROUPS):
        other = jnp.broadcast_to(grp[gi:gi + 1, :], (N_GROUPS, tm))
        grank = grank + ((other > grp) | ((other == grp) & (gi < gidx))).astype(jnp.int32)
    gsel = grank < TOPK_GROUPS
    emask = jnp.concatenate([jnp.broadcast_to(gsel[gi:gi + 1, :], (GROUP_SIZE, tm)) for gi in range(N_GROUPS)],
                            axis=0)
    masked = jnp.where(emask, biased, NEG_INF)
    eidx = lax.broadcasted_iota(jnp.int32, (N_EXPERTS, tm), 0)
    erank = jnp.zeros((N_EXPERTS, tm), jnp.int32)
    for e in range(N_EXPERTS):
        other = jnp.broadcast_to(masked[e:e + 1, :], (N_EXPERTS, tm))
        erank = erank + ((other > masked) | ((other == masked) & (e < eidx))).astype(jnp.int32)
    sel = jnp.where(erank < TOP_K, scores, 0.0)
    gates = sel / jnp.sum(sel, axis=0, keepdims=True) * ROUTED_SCALE
    return jnp.concatenate([gates, jnp.zeros_like(gates)], axis=0)


def _moe_body(x_ref, wrt_ref, eb_ref, wsg_ref, wsu_ref, wsd_ref, wg_ref, wu_ref, wd_ref, g3_ref, b3_ref,
              o_ref, xb_scr, gate_scr, acc_scr):
    e = pl.program_id(1)

    @pl.when(e == 0)
    def _():
        x = x_ref[...]
        xb = x.astype(BF16)
        xb_scr[...] = xb
        gate_scr[...] = jnp.transpose(_route(x, wrt_ref[...], eb_ref[...]))
        hs = jax.nn.silu(_dot(xb, wsg_ref[...])) * _dot(xb, wsu_ref[...])
        acc_scr[...] = _dot(hs.astype(BF16), wsd_ref[...])

    xb = xb_scr[...]
    hg = _dot(xb, wg_ref[0].astype(BF16))
    hu = _dot(xb, wu_ref[0].astype(BF16))
    lane = lax.broadcasted_iota(jnp.int32, gate_scr.shape, 1)
    gcol = jnp.sum(jnp.where(lane == e, gate_scr[...], 0.0), axis=1, keepdims=True)
    act = jax.nn.silu(hg) * hu * gcol
    acc_scr[...] = acc_scr[...] + _dot(act.astype(BF16), wd_ref[0].astype(BF16))

    @pl.when(e == pl.num_programs(1) - 1)
    def _():
        o_ref[...] = _layer_norm(DN_ALPHA * x_ref[...] + acc_scr[...], g3_ref[...], b3_ref[...])


def moe_ln(x, wr_t, e_bias, wsg, wsu, wsd, w_e_gate, w_e_up, w_e_down, g3, b3, tm):
    n = x.shape[0]
    row = pl.BlockSpec((tm, D_MODEL), lambda i, e: (i, 0))
    full = lambda a: pl.BlockSpec(a.shape, lambda i, e: (0,) * a.ndim)
    return pl.pallas_call(
        _moe_body,
        grid=(n // tm, N_EXPERTS),
        in_specs=[row, full(wr_t), full(e_bias), full(wsg), full(wsu), full(wsd),
                  pl.BlockSpec((1, D_MODEL, D_EXPERT), lambda i, e: (e, 0, 0)),
                  pl.BlockSpec((1, D_MODEL, D_EXPERT), lambda i, e: (e, 0, 0)),
                  pl.BlockSpec((1, D_EXPERT, D_MODEL), lambda i, e: (e, 0, 0)),
                  full(g3), full(b3)],
        out_specs=row,
        out_shape=jax.ShapeDtypeStruct((n, D_MODEL), F32),
        scratch_shapes=[pltpu.VMEM((tm, D_MODEL), BF16), pltpu.VMEM((tm, LANES), F32),
                        pltpu.VMEM((tm, D_MODEL), F32)],
        compiler_params=_cparams(("arbitrary", "arbitrary")),
        name="moe_ln3",
    )(x, wr_t, e_bias, wsg, wsu, wsd, w_e_gate, w_e_up, w_e_down, g3, b3)


def _block_ones(seg):
    idx = jnp.arange(LANES, dtype=jnp.int32) // seg
    return (idx[:, None] == idx[None, :]).astype(BF16)


def _row(v):
    return v.reshape(1, -1).astype(F32)


def kernel(x_prompt, x_sample, cache_k, cache_v, state_wkv, state_shift, cache_mem_k, cache_mem_v, page_table, mem_prompt, ln_in_g, ln_in_b, rel_bias, w_in, mu_shift, w0, w_decay_up, a0, w_aaa_up, w_gate_up, k_k, k_a, r_k, lnx_g, lnx_b, lam_q1, lam_k1, lam_q2, lam_k2, subln_g, w_out, ln1_g, ln1_b, w_mq, w_mk, w_mv, w_mo, ln2_g, ln2_b, w_router, e_bias, w_e_gate, w_e_up, w_e_down, w_s_gate, w_s_up, w_s_down, ln3_g, ln3_b):
    assert w_in.shape[0] == DEPTH == 1
    nb, t, _ = x_prompt.shape
    db, s_len, _ = x_sample.shape
    l = 0
    seg64 = _block_ones(RWKV_HD)
    ones128 = _block_ones(LANES)
    zeros = jnp.zeros((64, 512), F32)
    w_lo = jnp.concatenate([jnp.concatenate([w_decay_up[l], zeros], axis=1),
                            jnp.concatenate([zeros, w_aaa_up[l]], axis=1)], axis=0).astype(BF16)
    prep_consts = (_row(mu_shift[l]), w_lo, _row(jnp.concatenate([w0[l], a0[l]])), w_gate_up[l].astype(BF16),
                   _row(k_k[l]), _row(k_a[l]), _row(r_k[l]), seg64)
    mix_consts = (_row(lnx_g[l]), _row(lnx_b[l]), _row(jnp.tile(subln_g[l] * (1.0 - LAM_INIT), N_DIFF_HEADS)),
                  w_out[l].astype(BF16), _row(ln1_g[l]), _row(ln1_b[l]), seg64, ones128)
    lam = (jnp.exp(jnp.sum(lam_q1[l] * lam_k1[l]).astype(F32)) - jnp.exp(jnp.sum(lam_q2[l] * lam_k2[l]).astype(F32))
           + LAM_INIT).reshape(1, 1)
    w_in_b = w_in[l].astype(BF16)
    w_mq_b, w_mk_b, w_mv_b, w_mo_b = (w[l].astype(BF16) for w in (w_mq, w_mk, w_mv, w_mo))
    wr_t = jnp.transpose(w_router[l])
    eb = e_bias[l].reshape(N_EXPERTS, 1).astype(F32)
    wsg, wsu, wsd = w_s_gate[l].astype(BF16), w_s_up[l].astype(BF16), w_s_down[l].astype(BF16)
    g_in, b_in = _row(ln_in_g), _row(ln_in_b)
    g2, b2, g3, b3 = _row(ln2_g[l]), _row(ln2_b[l]), _row(ln3_g[l]), _row(ln3_b[l])

    def rwkv(p, prev, nseq, tlen, s0, tm, shift_in_kernel, tc):
        a, w, wr, b, k2, v, br, kr, g, bv = rwkv_prep(p, prev, prep_consts, tm, shift_in_kernel, tlen)
        y, st = wkv_scan(*[_to_scan_j(z, nseq, tlen) for z in (a, w, wr, b, k2)], _to_scan_i(v, nseq, tlen),
                         _to_scan_s(br, nseq, tlen), _to_scan_s(kr, nseq, tlen), s0, tc)
        return _from_scan_y(y, nseq, tlen), bv, g, _state_from_scan(st, nseq)

    def tail(h1, qm, mk, mv, nseq, tpad, tm, tm_moe):
        n = h1.shape[0]
        om = mem_attn(qm, mk, mv, min(tm, tpad))[:, :n // nseq].reshape(n, D_MODEL)
        h2 = matmul_res_ln(om, h1, w_mo_b, g2, b2, tm, "mem_out_ln2")
        return moe_ln(h2, wr_t, eb, wsg, wsu, wsd, w_e_gate[l], w_e_up[l], w_e_down[l], g3, b3, tm_moe)

    n_p = nb * t
    tm = 512
    hp, pp, qp, kp, vp = ln_proj(x_prompt.reshape(n_p, D_MODEL), g_in, b_in, w_in_b, tm)
    s0_p = jnp.zeros((nb * N_RWKV_HEADS // SCAN_BH, SCAN_JLO, RWKV_HD, LANES), F32)
    y_p, bv_p, gt_p, wkv_p = rwkv(pp, None, nb, t, s0_p, tm, True, 128)
    od_p = diff_attn_prompt(qp, kp, vp, rel_bias, lam, nb, t, 512)
    h1_p = mix_ln(y_p, bv_p, gt_p, od_p, hp, mix_consts, tm)
    mem_rows = mem_prompt.reshape(nb * N_MEM, D_MODEL)
    mk_p = matmul(mem_rows, w_mk_b, tm, "mem_k")
    mv_p = matmul(mem_rows, w_mv_b, tm, "mem_v")
    qm_p = matmul(h1_p, w_mq_b, tm, "mem_q").reshape(nb, t, D_MODEL)
    mk_p = mk_p.reshape(nb, N_MEM, N_MEM_HEADS, MEM_HD)
    mv_p = mv_p.reshape(nb, N_MEM, N_MEM_HEADS, MEM_HD)
    h3_p = tail(h1_p, qm_p, mk_p, mv_p, nb, t, tm, 1024)

    n_s = db * s_len
    hs, ps, qs, ks, vs = ln_proj(x_sample.reshape(n_s, D_MODEL), g_in, b_in, w_in_b, n_s)
    ps3 = ps.reshape(db, s_len, RWKV_COLS)
    prev_s = jnp.concatenate([state_shift[l][:, None, :], ps3[:, :-1]], axis=1).reshape(n_s, RWKV_COLS)
    y_s, bv_s, gt_s, wkv_s = rwkv(ps, prev_s, db, s_len, _state_to_scan(state_wkv[l], db), n_s, False, s_len)
    n_pool = cache_k.shape[1]
    od_s = diff_attn_sample(qs.reshape(db, s_len, 512), ks.reshape(db, s_len, 512), vs.reshape(db, s_len, 512),
                            jnp.transpose(cache_k[l], (0, 2, 3, 4, 1)).reshape(n_pool, DIFF_WIDTH, PAGE_SIZE),
                            cache_v[l].reshape(n_pool, PAGE_SIZE * N_DIFF_HEADS, DIFF_VD),
                            page_table, rel_bias, lam, 16)
    h1_s = mix_ln(y_s, bv_s, gt_s, od_s.reshape(n_s, DIFF_WIDTH), hs, mix_consts, n_s)
    qm_s = matmul(h1_s, w_mq_b, n_s, "mem_q").reshape(db, s_len, D_MODEL)
    qm_s = jnp.pad(qm_s, ((0, 0), (0, SUBLANES - s_len), (0, 0)))
    h3_s = tail(h1_s, qm_s, cache_mem_k[l], cache_mem_v[l], db, SUBLANES, n_s, n_s)

    return (h3_p.reshape(nb, t, D_MODEL), h3_s.reshape(db, s_len, D_MODEL),
            kp.reshape(1, nb, t, N_DIFF_HEADS, 2, DIFF_HD), vp.reshape(1, nb, t, N_DIFF_HEADS, DIFF_VD),
            wkv_p[None], pp.reshape(nb, t, RWKV_COLS)[:, -1][None],
            mk_p.reshape(1, nb, N_MEM, N_MEM_HEADS, MEM_HD), mv_p.reshape(1, nb, N_MEM, N_MEM_HEADS, MEM_HD),
            ks.reshape(1, db, s_len, N_DIFF_HEADS, 2, DIFF_HD), vs.reshape(1, db, s_len, N_DIFF_HEADS, DIFF_VD),
            wkv_s[None], ps3[:, -1][None])
```

```python
import functools
import math

import jax
import jax.numpy as jnp
from jax import lax
from jax.experimental import pallas as pl
from jax.experimental.pallas import tpu as pltpu

F32 = jnp.float32
BF16 = jnp.bfloat16

D_MODEL = 1024
RWKV_WIDTH = 512
RWKV_HD = 64
N_RWKV_HEADS = 8
RWKV_COLS = 1792
GN_EPS = 64e-5
DIFF_WIDTH = 512
DIFF_HD = 64
DIFF_VD = 128
N_DIFF_HEADS = 4
RMS_EPS = 1e-5
N_BUCKETS = 32
MAX_DISTANCE = 128
PAGE_SIZE = 128
N_MEM = 256
N_MEM_HEADS = 4
MEM_HD = 256
N_EXPERTS = 64
N_GROUPS = 8
GROUP_SIZE = 8
TOPK_GROUPS = 4
TOP_K = 8
D_EXPERT = 256
ROUTED_SCALE = 2.5
MOE_EXPERTS_PER_STEP = 2
DEPTH = 1
DN_ALPHA = (2.0 * DEPTH) ** 0.25
LN_EPS = 1e-5
NEG_INF = -1e30
LAM_INIT = 0.8 - 0.6 * math.exp(-0.3 * 0)

LANES = 128
SUBLANES = 8
VMEM_LIMIT = 56 * 1024 * 1024

SCAN_JLO = 16
SCAN_JHI = RWKV_HD // SCAN_JLO
SCAN_BH = LANES // SCAN_JHI


def _cparams(sem):
    return pltpu.CompilerParams(dimension_semantics=sem, vmem_limit_bytes=VMEM_LIMIT)


def _layer_norm(x, g, b):
    mu = jnp.mean(x, -1, keepdims=True)
    d = x - mu
    var = jnp.mean(d * d, -1, keepdims=True)
    return d * lax.rsqrt(var + LN_EPS) * g + b


def _dot(a, b):
    return jnp.dot(a, b, preferred_element_type=F32)


def _dot_nt(a, b):
    return lax.dot_general(a, b, (((1,), (1,)), ((), ())), preferred_element_type=F32)


def _seg_sum(x, ones_blk):
    outs = []
    for c in range(x.shape[1] // LANES):
        xc = x[:, c * LANES:(c + 1) * LANES]
        hi = xc.astype(BF16)
        lo = (xc - hi.astype(F32)).astype(BF16)
        outs.append(_dot(hi, ones_blk) + _dot(lo, ones_blk))
    return outs[0] if len(outs) == 1 else jnp.concatenate(outs, axis=1)


def _ln_proj_body(k_transposed, x_ref, g_ref, b_ref, w_ref, wkt_ref, h_ref, p_ref, q_ref, k_ref, v_ref):
    h = _layer_norm(x_ref[...], g_ref[...], b_ref[...])
    h_ref[...] = h
    hb = h.astype(BF16)
    c0 = RWKV_COLS
    p_ref[...] = _dot(hb, w_ref[:, 0:c0])
    q_ref[...] = _dot(hb, w_ref[:, c0:c0 + 512])
    if k_transposed:
        k_ref[0] = _dot_nt(wkt_ref[...], hb)
    else:
        k_ref[...] = _dot(hb, w_ref[:, c0 + 512:c0 + 1024])
    v_ref[...] = _dot(hb, w_ref[:, c0 + 1024:c0 + 1536])


def ln_proj(x, g, b, w_bf16, wkt_bf16, tm, seq_len=None):
    n = x.shape[0]
    row = lambda width: pl.BlockSpec((tm, width), lambda i: (i, 0))
    full = lambda a: pl.BlockSpec(a.shape, lambda i: (0,) * a.ndim)
    if seq_len is None:
        k_spec, k_shape = row(512), (n, 512)
    else:
        bps = seq_len // tm
        k_spec = pl.BlockSpec((1, 512, tm), lambda i: (i // bps, 0, i % bps))
        k_shape = (n // seq_len, 512, seq_len)
    shapes = [(n, D_MODEL), (n, RWKV_COLS), (n, 512), k_shape, (n, 512)]
    return pl.pallas_call(
        functools.partial(_ln_proj_body, seq_len is not None),
        grid=(n // tm,),
        in_specs=[row(D_MODEL), full(g), full(b), full(w_bf16), full(wkt_bf16)],
        out_specs=[row(D_MODEL), row(RWKV_COLS), row(512), k_spec, row(512)],
        out_shape=[jax.ShapeDtypeStruct(sh, F32) for sh in shapes],
        compiler_params=_cparams(("arbitrary",)),
        name="ln_proj",
    )(x, g, b, w_bf16, wkt_bf16)


def _mm_body(x_ref, w_ref, o_ref):
    o_ref[...] = _dot(x_ref[...].astype(BF16), w_ref[...])


def matmul(x, w_bf16, tm, name):
    n, kdim = x.shape
    nout = w_bf16.shape[1]
    return pl.pallas_call(
        _mm_body,
        grid=(n // tm,),
        in_specs=[pl.BlockSpec((tm, kdim), lambda i: (i, 0)), pl.BlockSpec((kdim, nout), lambda i: (0, 0))],
        out_specs=pl.BlockSpec((tm, nout), lambda i: (i, 0)),
        out_shape=jax.ShapeDtypeStruct((n, nout), F32),
        compiler_params=_cparams(("arbitrary",)),
        name=name,
    )(x, w_bf16)


def _mm_res_ln_body(x_ref, res_ref, w_ref, g_ref, b_ref, o_ref):
    y = _dot(x_ref[...].astype(BF16), w_ref[...])
    o_ref[...] = _layer_norm(DN_ALPHA * res_ref[...] + y, g_ref[...], b_ref[...])


def matmul_res_ln(x, res, w_bf16, g, b, tm, name):
    n, kdim = x.shape
    nout = w_bf16.shape[1]
    row = lambda width: pl.BlockSpec((tm, width), lambda i: (i, 0))
    full = lambda a: pl.BlockSpec(a.shape, lambda i: (0,) * a.ndim)
    return pl.pallas_call(
        _mm_res_ln_body,
        grid=(n // tm,),
        in_specs=[row(kdim), row(nout), full(w_bf16), full(g), full(b)],
        out_specs=row(nout),
        out_shape=jax.ShapeDtypeStruct((n, nout), F32),
        compiler_params=_cparams(("arbitrary",)),
        name=name,
    )(x, res, w_bf16, g, b)


def _rwkv_prep_body(shift_in_kernel, seq_blocks, p_ref, prev_ref, mu_ref, wlo_ref, w0a0_ref, wg_ref,
                    kk_ref, ka_ref, rk_ref, seg_ref,
                    a_o, w_o, wr_o, b_o, k_o, v_o, br_o, kr_o, g_o, bv_o):
    p = p_ref[...]
    if shift_in_kernel:
        first = jnp.where(pl.program_id(0) % seq_blocks == 0, 0.0, prev_ref[SUBLANES - 1:SUBLANES, :])
        row = lax.broadcasted_iota(jnp.int32, p.shape, 0)
        p_prev = jnp.where(row == 0, first, pltpu.roll(p, 1, 0))
    else:
        p_prev = prev_ref[...]
    ps = p + (p_prev - p) * mu_ref[...]
    r = ps[:, 0:512]
    k = ps[:, 512:1024]
    v = ps[:, 1024:1536]
    lo = ps[:, 1536:1664]
    g_lo = ps[:, 1664:1792]
    lane = lax.broadcasted_iota(jnp.int32, lo.shape, 1)
    lo = jnp.where(lane < 64, jnp.tanh(lo), lo)
    wa = _dot(lo.astype(BF16), wlo_ref[...]) + w0a0_ref[...]
    x = -wa[:, 0:512]
    softplus = jnp.maximum(x, 0.0) + jnp.log(1.0 + jnp.exp(-jnp.abs(x)))
    decay = jnp.exp(-jnp.exp(-softplus - 0.5))
    a = jax.nn.sigmoid(wa[:, 512:1024])
    g = _dot(jax.nn.sigmoid(g_lo).astype(BF16), wg_ref[...])
    seg = seg_ref[...]
    kk = k * kk_ref[...]
    kk = kk / jnp.maximum(jnp.sqrt(_seg_sum(kk * kk, seg)), 1e-12)
    k2 = k * (1.0 + (a - 1.0) * ka_ref[...])
    b = kk * a
    a_o[...] = -kk
    w_o[...] = decay
    wr_o[...] = decay * r
    b_o[...] = b
    k_o[...] = k2
    v_o[...] = v
    br_o[...] = _seg_sum(b * r, seg)
    kr_o[...] = _seg_sum(k2 * r, seg)
    g_o[...] = g
    bv_o[...] = _seg_sum(r * k2 * rk_ref[...], seg) * v


def rwkv_prep(p, prev, consts, tm, shift_in_kernel, seq_len):
    n = p.shape[0]
    row = lambda width: pl.BlockSpec((tm, width), lambda i: (i, 0))
    full = lambda a: pl.BlockSpec(a.shape, lambda i: (0,) * a.ndim)
    if shift_in_kernel:
        per8 = tm // SUBLANES
        prev_spec = pl.BlockSpec((SUBLANES, RWKV_COLS), lambda i: (jnp.maximum(i * per8 - 1, 0), 0))
        prev = p
    else:
        prev_spec = row(RWKV_COLS)
    body = functools.partial(_rwkv_prep_body, shift_in_kernel, max(seq_len // tm, 1))
    return pl.pallas_call(
        body,
        grid=(n // tm,),
        in_specs=[row(RWKV_COLS), prev_spec] + [full(c) for c in consts],
        out_specs=[row(512)] * 10,
        out_shape=[jax.ShapeDtypeStruct((n, 512), F32)] * 10,
        compiler_params=_cparams(("arbitrary",)),
        name="rwkv_prep",
    )(p, prev, *consts)


def _lane_fold(x):
    return x + pltpu.roll(x, 32, 1) + pltpu.roll(x, 64, 1) + pltpu.roll(x, 96, 1)


def _wkv_scan_body(tc, a_ref, w_ref, wr_ref, b_ref, k_ref, v_ref, br_ref, kr_ref, s0_ref,
                   y_ref, st_ref, s_scr):
    c = pl.program_id(1)
    ngrp = RWKV_HD // SUBLANES

    @pl.when(c == 0)
    def _():
        s_scr[...] = s0_ref[0]

    def bcast(ref, t, j):
        return jnp.broadcast_to(ref[0, t, pl.ds(j, 1), :], (SUBLANES, LANES))

    def step(t, carry):
        acc_a = [jnp.zeros((SUBLANES, LANES), F32) for _ in range(ngrp)]
        acc_y = [jnp.zeros((SUBLANES, LANES), F32) for _ in range(ngrp)]
        for j in range(SCAN_JLO):
            ab = bcast(a_ref, t, j)
            wrb = bcast(wr_ref, t, j)
            for ig in range(ngrp):
                s = s_scr[j, ig * SUBLANES:(ig + 1) * SUBLANES, :]
                acc_a[ig] = acc_a[ig] + s * ab
                acc_y[ig] = acc_y[ig] + s * wrb
        sa = [_lane_fold(x) for x in acc_a]
        vv = [v_ref[0, t, ig * SUBLANES:(ig + 1) * SUBLANES, :] for ig in range(ngrp)]
        for j in range(SCAN_JLO):
            wb = bcast(w_ref, t, j)
            bb = bcast(b_ref, t, j)
            kb = bcast(k_ref, t, j)
            for ig in range(ngrp):
                sl = slice(ig * SUBLANES, (ig + 1) * SUBLANES)
                s_scr[j, sl, :] = s_scr[j, sl, :] * wb + sa[ig] * bb + vv[ig] * kb
        br = jnp.broadcast_to(br_ref[0, t], (SUBLANES, LANES))
        kr = jnp.broadcast_to(kr_ref[0, t], (SUBLANES, LANES))
        for ig in range(ngrp):
            y_ref[0, t, ig * SUBLANES:(ig + 1) * SUBLANES, :] = _lane_fold(acc_y[ig]) + sa[ig] * br + vv[ig] * kr
        return carry

    lax.fori_loop(0, tc, step, 0)

    @pl.when(c == pl.num_programs(1) - 1)
    def _():
        st_ref[0] = s_scr[...]


def wkv_scan(a, w, wr, b, k, v, br, kr, s0, tc):
    g, t = a.shape[0], a.shape[1]
    jspec = pl.BlockSpec((1, tc, SCAN_JLO, LANES), lambda gi, c: (gi, c, 0, 0))
    ispec = pl.BlockSpec((1, tc, RWKV_HD, LANES), lambda gi, c: (gi, c, 0, 0))
    sspec = pl.BlockSpec((1, tc, 1, LANES), lambda gi, c: (gi, c, 0, 0))
    stspec = pl.BlockSpec((1, SCAN_JLO, RWKV_HD, LANES), lambda gi, c: (gi, 0, 0, 0))
    return pl.pallas_call(
        functools.partial(_wkv_scan_body, tc),
        grid=(g, t // tc),
        in_specs=[jspec] * 5 + [ispec, sspec, sspec, stspec],
        out_specs=[ispec, stspec],
        out_shape=[jax.ShapeDtypeStruct((g, t, RWKV_HD, LANES), F32),
                   jax.ShapeDtypeStruct((g, SCAN_JLO, RWKV_HD, LANES), F32)],
        scratch_shapes=[pltpu.VMEM((SCAN_JLO, RWKV_HD, LANES), F32)],
        compiler_params=_cparams(("arbitrary", "arbitrary")),
        name="wkv_scan",
    )(a, w, wr, b, k, v, br, kr, s0)


def _to_scan_j(x, nb, t):
    g = nb * N_RWKV_HEADS // SCAN_BH
    bp = nb // g
    x = x.reshape(g, bp, t, N_RWKV_HEADS, SCAN_JHI, SCAN_JLO)
    return jnp.transpose(x, (0, 2, 5, 4, 1, 3)).reshape(g, t, SCAN_JLO, LANES)


def _to_scan_i(x, nb, t):
    g = nb * N_RWKV_HEADS // SCAN_BH
    bp = nb // g
    x = x.reshape(g, bp, t, N_RWKV_HEADS, RWKV_HD)
    x = jnp.transpose(x, (0, 2, 4, 1, 3)).reshape(g, t, RWKV_HD, 1, SCAN_BH)
    return jnp.broadcast_to(x, (g, t, RWKV_HD, SCAN_JHI, SCAN_BH)).reshape(g, t, RWKV_HD, LANES)


def _to_scan_s(x, nb, t):
    g = nb * N_RWKV_HEADS // SCAN_BH
    bp = nb // g
    x = x.reshape(g, bp, t, N_RWKV_HEADS, RWKV_HD)[..., 0]
    x = jnp.transpose(x, (0, 2, 1, 3)).reshape(g, t, 1, 1, SCAN_BH)
    return jnp.broadcast_to(x, (g, t, 1, SCAN_JHI, SCAN_BH)).reshape(g, t, 1, LANES)


def _from_scan_y(y, nb, t):
    g = y.shape[0]
    bp = nb // g
    y = y[..., :SCAN_BH].reshape(g, t, RWKV_HD, bp, N_RWKV_HEADS)
    return jnp.transpose(y, (0, 3, 1, 4, 2)).reshape(nb * t, RWKV_WIDTH)


def _state_to_scan(s, nb):
    g = nb * N_RWKV_HEADS // SCAN_BH
    bp = nb // g
    s = s.reshape(g, bp, N_RWKV_HEADS, RWKV_HD, SCAN_JHI, SCAN_JLO)
    return jnp.transpose(s, (0, 5, 3, 4, 1, 2)).reshape(g, SCAN_JLO, RWKV_HD, LANES)


def _state_from_scan(s, nb):
    g = s.shape[0]
    bp = nb // g
    s = s.reshape(g, SCAN_JLO, RWKV_HD, SCAN_JHI, bp, N_RWKV_HEADS)
    return jnp.transpose(s, (0, 4, 5, 2, 3, 1)).reshape(nb, N_RWKV_HEADS, RWKV_HD, RWKV_HD)


def _mix_body(y_ref, bv_ref, g_ref, od_ref, h_ref, lnxg_ref, lnxb_ref, sub_ref, wout_ref, g1_ref, b1_ref,
              seg_ref, ones_ref, o_ref):
    seg = seg_ref[...]
    y = y_ref[...]
    mu = _seg_sum(y, seg) * (1.0 / RWKV_HD)
    d = y - mu
    var = _seg_sum(d * d, seg) * (1.0 / RWKV_HD)
    yn = d * lax.rsqrt(var + GN_EPS) * lnxg_ref[...] + lnxb_ref[...]
    o_rw = (yn + bv_ref[...]) * g_ref[...]
    od = od_ref[...]
    ms = _seg_sum(od * od, ones_ref[...]) * (1.0 / DIFF_VD)
    o_df = od * lax.rsqrt(ms + RMS_EPS) * sub_ref[...]
    mix = _dot(o_rw.astype(BF16), wout_ref[0:512, :]) + _dot(o_df.astype(BF16), wout_ref[512:1024, :])
    o_ref[...] = _layer_norm(DN_ALPHA * h_ref[...] + mix, g1_ref[...], b1_ref[...])


def mix_ln(y, bv, g, od, h, consts, tm):
    n = y.shape[0]
    row = lambda width: pl.BlockSpec((tm, width), lambda i: (i, 0))
    full = lambda a: pl.BlockSpec(a.shape, lambda i: (0,) * a.ndim)
    return pl.pallas_call(
        _mix_body,
        grid=(n // tm,),
        in_specs=[row(512)] * 4 + [row(D_MODEL)] + [full(c) for c in consts],
        out_specs=row(D_MODEL),
        out_shape=jax.ShapeDtypeStruct((n, D_MODEL), F32),
        compiler_params=_cparams(("arbitrary",)),
        name="mix_ln1",
    )(y, bv, g, od, h, *consts)


def _t5_bucket(dist):
    max_exact = N_BUCKETS // 2
    d = jnp.maximum(dist, 1).astype(F32)
    large = max_exact + (jnp.log(d / max_exact) / math.log(MAX_DISTANCE / max_exact)
                         * (N_BUCKETS - max_exact)).astype(jnp.int32)
    return jnp.where(dist < max_exact, dist, jnp.minimum(large, N_BUCKETS - 1))


def _bias_table(rel_bias, dist):
    bucket = _t5_bucket(jnp.maximum(dist, 0))
    tail = (1,) * dist.ndim
    bias = jnp.zeros(rel_bias.shape[1:] + dist.shape, F32)
    for bk in range(N_BUCKETS):
        bias = jnp.where(bucket == bk, rel_bias[bk].astype(F32).reshape(rel_bias.shape[1:] + tail), bias)
    return jnp.where(dist >= 0, bias, NEG_INF)


def _dap_body(tq, tk, qi_ref, ki_ref, lam_ref, q_ref, k_ref, v_ref, bias_ref, cfar_ref, o_ref,
              qs_scr, m_scr, l_scr, acc_scr):
    step = pl.program_id(2)
    qi = qi_ref[step]
    ki = ki_ref[step]

    @pl.when(ki == 0)
    def _():
        q = q_ref[...] * (DIFF_HD ** -0.5)
        lane = lax.broadcasted_iota(jnp.int32, q.shape, 1)
        qs_scr[0:tq, :] = jnp.where(lane < DIFF_HD, q, 0.0).astype(BF16)
        qs_scr[tq:2 * tq, :] = jnp.where(lane >= DIFF_HD, q, 0.0).astype(BF16)
        m_scr[...] = jnp.full(m_scr.shape, NEG_INF, F32)
        l_scr[...] = jnp.zeros(l_scr.shape, F32)
        acc_scr[...] = jnp.zeros(acc_scr.shape, F32)

    def update(s, shift):
        m_prev = m_scr[...]
        m_new = jnp.maximum(m_prev, jnp.max(s, axis=1, keepdims=True) + shift)
        alpha = jnp.exp(m_prev - m_new)
        p = jnp.exp(s - jnp.tile(m_new - shift, (1, tk // LANES)))
        l_scr[...] = alpha * l_scr[...] + jnp.sum(p, axis=1, keepdims=True)
        acc_scr[...] = alpha * acc_scr[...] + _dot(p.astype(BF16), v_ref[...].astype(BF16))
        m_scr[...] = m_new

    @pl.when(ki < qi - 1)
    def _():
        update(_dot(qs_scr[...], k_ref[0].astype(BF16)), cfar_ref[0])

    @pl.when(ki >= qi - 1)
    def _():
        update(_dot(qs_scr[...], k_ref[0].astype(BF16)) + bias_ref[0, 0], jnp.zeros((2 * tq, LANES), F32))

    @pl.when(ki == qi)
    def _():
        o = acc_scr[...] / l_scr[...]
        o_ref[...] = o[0:tq] - lam_ref[0, 0] * o[tq:2 * tq]


def diff_attn_prompt(q, kt, v, rel_bias, lam, nb, t, tq):
    tk = tq
    assert tq >= MAX_DISTANCE and t % tq == 0
    nq = t // tq
    qi_list, ki_list = [], []
    for a in range(nq):
        for c in range(a + 1):
            qi_list.append(a)
            ki_list.append(c)
    qi_arr = jnp.asarray(qi_list, jnp.int32)
    ki_arr = jnp.asarray(ki_list, jnp.int32)
    rr = jnp.arange(tq, dtype=jnp.int32)[:, None]
    cc = jnp.arange(tk, dtype=jnp.int32)[None, :]
    tiles = jnp.stack([_bias_table(rel_bias, tq + rr - cc), _bias_table(rel_bias, rr - cc)], axis=1)
    tiles = tiles.reshape(N_DIFF_HEADS, 2, 2 * tq, tk)
    cfar = _bias_table(rel_bias, jnp.full((tq, LANES), 2 * MAX_DISTANCE, jnp.int32))
    cfar = cfar.reshape(N_DIFF_HEADS, 2 * tq, LANES)

    grid_spec = pltpu.PrefetchScalarGridSpec(
        num_scalar_prefetch=2,
        grid=(nb, N_DIFF_HEADS, len(qi_list)),
        in_specs=[
            pl.BlockSpec(memory_space=pltpu.SMEM),
            pl.BlockSpec((tq, LANES), lambda b, h, s, qi, ki: (b * nq + qi[s], h)),
            pl.BlockSpec((1, LANES, tk), lambda b, h, s, qi, ki: (b, h, ki[s])),
            pl.BlockSpec((tk, LANES), lambda b, h, s, qi, ki: (b * nq + ki[s], h)),
            pl.BlockSpec((1, 1, 2 * tq, tk), lambda b, h, s, qi, ki: (h, jnp.where(ki[s] == qi[s], 1, 0), 0, 0)),
            pl.BlockSpec((1, 2 * tq, LANES), lambda b, h, s, qi, ki: (h, 0, 0)),
        ],
        out_specs=pl.BlockSpec((tq, LANES), lambda b, h, s, qi, ki: (b * nq + qi[s], h)),
        scratch_shapes=[pltpu.VMEM((2 * tq, LANES), BF16), pltpu.VMEM((2 * tq, LANES), F32),
                        pltpu.VMEM((2 * tq, LANES), F32), pltpu.VMEM((2 * tq, LANES), F32)],
    )
    return pl.pallas_call(
        functools.partial(_dap_body, tq, tk),
        grid_spec=grid_spec,
        out_shape=jax.ShapeDtypeStruct((nb * t, DIFF_WIDTH), F32),
        compiler_params=_cparams(("arbitrary", "arbitrary", "arbitrary")),
        name="diff_attn_prompt",
    )(qi_arr, ki_arr, lam, q, kt, v, tiles, cfar)


def _das_body(npg, s_len, pt_ref, lam_ref, q_ref, qmask_ref, bias_ref, bnew_ref, kn_ref, vn_ref, *rest):
    k_refs = rest[0:npg]
    v_refs = rest[npg:2 * npg]
    o_ref = rest[2 * npg]
    qs_scr, m_scr, l_scr, acc_scr = rest[2 * npg + 1:]
    g = pl.program_id(1)
    rows = 2 * N_DIFF_HEADS * s_len

    @pl.when(g == 0)
    def _():
        q = q_ref[0] * (DIFF_HD ** -0.5)
        qrep = jnp.concatenate([q] * (2 * N_DIFF_HEADS), axis=0)
        qs_scr[...] = (qrep * qmask_ref[...]).astype(BF16)
        m_scr[...] = jnp.full(m_scr.shape, NEG_INF, F32)
        l_scr[...] = jnp.zeros(l_scr.shape, F32)
        acc_scr[...] = jnp.zeros(acc_scr.shape, F32)

    hrows = 2 * s_len
    qs = qs_scr[...]
    s = jnp.concatenate([_dot(qs, kr[0].astype(BF16)) for kr in k_refs], axis=1) + bias_ref[0]
    m_prev = m_scr[...]
    m_new = jnp.maximum(m_prev, jnp.max(s, axis=1, keepdims=True))
    alpha = jnp.exp(m_prev - m_new)
    p = jnp.exp(s - jnp.tile(m_new, (1, npg)))
    l_scr[...] = alpha * l_scr[...] + jnp.sum(p, axis=1, keepdims=True)
    pvs = []
    for h in range(N_DIFF_HEADS):
        pv = None
        for j in range(npg):
            ph = p[h * hrows:(h + 1) * hrows, j * PAGE_SIZE:(j + 1) * PAGE_SIZE].astype(BF16)
            vh = v_refs[j][0, pl.ds(h, PAGE_SIZE, stride=N_DIFF_HEADS), :].astype(BF16)
            d = _dot(ph, vh)
            pv = d if pv is None else pv + d
        pvs.append(pv)
    acc_scr[...] = alpha * acc_scr[...] + jnp.concatenate(pvs, axis=0)
    m_scr[...] = m_new

    @pl.when(g == pl.num_programs(1) - 1)
    def _():
        qf = qs_scr[...].astype(F32)
        kn = kn_ref[0].astype(BF16).astype(F32)
        vn = vn_ref[0].astype(BF16).astype(F32)
        lane = lax.broadcasted_iota(jnp.int32, (rows, LANES), 1)
        sn = bnew_ref[...]
        for j in range(s_len):
            dj = jnp.sum(qf * kn[j:j + 1, :], axis=1, keepdims=True)
            sn = sn + jnp.where(lane == j, dj, 0.0)
        m_prev = m_scr[...]
        m_new = jnp.maximum(m_prev, jnp.max(sn, axis=1, keepdims=True))
        alpha = jnp.exp(m_prev - m_new)
        pn = jnp.exp(sn - m_new).astype(BF16).astype(F32)
        l_fin = alpha * l_scr[...] + jnp.sum(pn, axis=1, keepdims=True)
        acc = alpha * acc_scr[...]
        for j in range(s_len):
            vj = jnp.concatenate([jnp.broadcast_to(vn[j:j + 1, h * DIFF_VD:(h + 1) * DIFF_VD], (hrows, DIFF_VD))
                                  for h in range(N_DIFF_HEADS)], axis=0)
            acc = acc + pn[:, j:j + 1] * vj
        o = acc / l_fin
        lam = lam_ref[0, 0]
        outs = []
        for h in range(N_DIFF_HEADS):
            blk = o[h * hrows:(h + 1) * hrows, :]
            outs.append(blk[0:s_len] - lam * blk[s_len:hrows])
        o_ref[0] = jnp.concatenate(outs, axis=1)


def diff_attn_sample(q, k, v, cache_k, cache_v, page_table, rel_bias, lam, npg):
    db, s_len, _ = q.shape
    n_pages = page_table.shape[1]
    past = n_pages * PAGE_SIZE
    assert n_pages % npg == 0
    ngroups = n_pages // npg
    rows = 2 * N_DIFF_HEADS * s_len
    width = npg * PAGE_SIZE
    hm = jnp.arange(rows, dtype=jnp.int32) // s_len
    qmask = (jnp.arange(DIFF_WIDTH, dtype=jnp.int32)[None, :] // DIFF_HD == hm[:, None]).astype(F32)
    qpos = past + jnp.arange(s_len, dtype=jnp.int32)

    def rows_of(table):
        return table.reshape(rows, table.shape[-1])

    far = rows_of(_bias_table(rel_bias, jnp.full((s_len, width), 2 * MAX_DISTANCE, jnp.int32)))
    kpos_last = past - width + jnp.arange(width, dtype=jnp.int32)
    last = rows_of(_bias_table(rel_bias, qpos[:, None] - kpos_last[None, :]))
    bias = jnp.stack([far, last], axis=0)
    knew_pos = past + jnp.arange(LANES, dtype=jnp.int32)
    dist_new = jnp.where(jnp.arange(LANES)[None, :] < s_len, qpos[:, None] - knew_pos[None, :], -1)
    bnew = rows_of(_bias_table(rel_bias, dist_new))
    assert width >= MAX_DISTANCE + s_len
    pt_flat = page_table.reshape(-1).astype(jnp.int32)

    def page_spec(j):
        return pl.BlockSpec((1, DIFF_WIDTH, PAGE_SIZE),
                            lambda b, g, pt, j=j: (pt[b * n_pages + g * npg + j], 0, 0))

    seq_spec = pl.BlockSpec((1, s_len, DIFF_WIDTH), lambda b, g, pt: (b, 0, 0))
    full2 = lambda a: pl.BlockSpec(a.shape, lambda b, g, pt: (0,) * a.ndim)
    grid_spec = pltpu.PrefetchScalarGridSpec(
        num_scalar_prefetch=1,
        grid=(db, ngroups),
        in_specs=[pl.BlockSpec(memory_space=pltpu.SMEM), seq_spec, full2(qmask),
                  pl.BlockSpec((1, rows, width), lambda b, g, pt: (jnp.where(g == ngroups - 1, 1, 0), 0, 0)),
                  full2(bnew), seq_spec, seq_spec]
                 + [page_spec(j) for j in range(npg)] + [page_spec(j) for j in range(npg)],
        out_specs=seq_spec,
        scratch_shapes=[pltpu.VMEM((rows, DIFF_WIDTH), BF16), pltpu.VMEM((rows, LANES), F32),
                        pltpu.VMEM((rows, LANES), F32), pltpu.VMEM((rows, DIFF_VD), F32)],
    )
    return pl.pallas_call(
        functools.partial(_das_body, npg, s_len),
        grid_spec=grid_spec,
        out_shape=jax.ShapeDtypeStruct((db, s_len, DIFF_WIDTH), F32),
        compiler_params=_cparams(("arbitrary", "arbitrary")),
        name="diff_attn_sample",
    )(pt_flat, lam, q, qmask, bias, bnew, k, v, *([cache_k] * npg), *([cache_v] * npg))


def _mem_attn_body(q_ref, mk_ref, mv_ref, o_ref):
    for h in range(N_MEM_HEADS):
        cols = slice(h * MEM_HD, (h + 1) * MEM_HD)
        qh = (q_ref[0, :, cols] * (MEM_HD ** -0.5)).astype(BF16)
        s = _dot_nt(qh, mk_ref[0, :, cols].astype(BF16))
        p = jnp.exp(s - jnp.max(s, axis=1, keepdims=True))
        l = jnp.sum(p, axis=1, keepdims=True)
        o_ref[0, :, cols] = _dot(p.astype(BF16), mv_ref[0, :, cols].astype(BF16)) / l


def mem_attn(q, mk, mv, tm):
    nb, t, _ = q.shape
    qspec = pl.BlockSpec((1, tm, D_MODEL), lambda b, i: (b, i, 0))
    mspec = pl.BlockSpec((1, N_MEM, D_MODEL), lambda b, i: (b, 0, 0))
    return pl.pallas_call(
        _mem_attn_body,
        grid=(nb, t // tm),
        in_specs=[qspec, mspec, mspec],
        out_specs=qspec,
        out_shape=jax.ShapeDtypeStruct((nb, t, D_MODEL), F32),
        compiler_params=_cparams(("arbitrary", "arbitrary")),
        name="mem_attn",
    )(q, mk, mv)


def _route(x, wr_t, e_bias):
    tm = x.shape[0]
    logits = lax.dot_general(wr_t, x, (((1,), (1,)), ((), ())), precision=lax.Precision.HIGHEST,
                             preferred_element_type=F32)
    scores = jax.nn.sigmoid(logits)
    biased = scores + e_bias
    sub = lax.broadcasted_iota(jnp.int32, (GROUP_SIZE, tm), 0)
    grp_rows = []
    for gi in range(N_GROUPS):
        xg = biased[gi * GROUP_SIZE:(gi + 1) * GROUP_SIZE, :]
        m1 = jnp.max(xg, axis=0, keepdims=True)
        i1 = jnp.min(jnp.where(xg == m1, sub, GROUP_SIZE), axis=0, keepdims=True)
        m2 = jnp.max(jnp.where(sub == i1, -jnp.inf, xg), axis=0, keepdims=True)
        grp_rows.append(m1 + m2)
    grp = jnp.concatenate(grp_rows, axis=0)
    gidx = lax.broadcasted_iota(jnp.int32, (N_GROUPS, tm), 0)
    grank = jnp.zeros((N_GROUPS, tm), jnp.int32)
    for gi in range(N_GROUPS):
        other = jnp.broadcast_to(grp[gi:gi + 1, :], (N_GROUPS, tm))
        grank = grank + ((other > grp) | ((other == grp) & (gi < gidx))).astype(jnp.int32)
    gsel = grank < TOPK_GROUPS
    emask = jnp.concatenate([jnp.broadcast_to(gsel[gi:gi + 1, :], (GROUP_SIZE, tm)) for gi in range(N_GROUPS)],
                            axis=0)
    masked = jnp.where(emask, biased, NEG_INF)
    eidx = lax.broadcasted_iota(jnp.int32, (N_EXPERTS, tm), 0)
    erank = jnp.zeros((N_EXPERTS, tm), jnp.int32)
    for e in range(N_EXPERTS):
        other = jnp.broadcast_to(masked[e:e + 1, :], (N_EXPERTS, tm))
        erank = erank + ((other > masked) | ((other == masked) & (e < eidx))).astype(jnp.int32)
    sel = jnp.where(erank < TOP_K, scores, 0.0)
    gates = sel / jnp.sum(sel, axis=0, keepdims=True) * ROUTED_SCALE
    return jnp.concatenate([gates, jnp.zeros_like(gates)], axis=0)


def _moe_body(x_ref, wrt_ref, eb_ref, wsg_ref, wsu_ref, wsd_ref, wg_ref, wu_ref, wd_ref, g3_ref, b3_ref,
              o_ref, xb_scr, gate_scr, acc_scr):
    e = pl.program_id(1)

    @pl.when(e == 0)
    def _():
        x = x_ref[...]
        xb = x.astype(BF16)
        xb_scr[...] = xb
        gate_scr[...] = jnp.transpose(_route(x, wrt_ref[...], eb_ref[...]))
        hs = jax.nn.silu(_dot(xb, wsg_ref[...])) * _dot(xb, wsu_ref[...])
        acc_scr[...] = _dot(hs.astype(BF16), wsd_ref[...])

    xb = xb_scr[...]
    lane = lax.broadcasted_iota(jnp.int32, gate_scr.shape, 1)
    acc = acc_scr[...]
    for j in range(MOE_EXPERTS_PER_STEP):
        hg = _dot(xb, wg_ref[j].astype(BF16))
        hu = _dot(xb, wu_ref[j].astype(BF16))
        gcol = jnp.sum(jnp.where(lane == e * MOE_EXPERTS_PER_STEP + j, gate_scr[...], 0.0), axis=1, keepdims=True)
        act = jax.nn.silu(hg) * hu * gcol
        acc = acc + _dot(act.astype(BF16), wd_ref[j].astype(BF16))
    acc_scr[...] = acc

    @pl.when(e == pl.num_programs(1) - 1)
    def _():
        o_ref[...] = _layer_norm(DN_ALPHA * x_ref[...] + acc_scr[...], g3_ref[...], b3_ref[...])


def moe_ln(x, wr_t, e_bias, wsg, wsu, wsd, w_e_gate, w_e_up, w_e_down, g3, b3, tm):
    n = x.shape[0]
    eps = MOE_EXPERTS_PER_STEP
    row = pl.BlockSpec((tm, D_MODEL), lambda i, e: (i, 0))
    full = lambda a: pl.BlockSpec(a.shape, lambda i, e: (0,) * a.ndim)
    return pl.pallas_call(
        _moe_body,
        grid=(n // tm, N_EXPERTS // eps),
        in_specs=[row, full(wr_t), full(e_bias), full(wsg), full(wsu), full(wsd),
                  pl.BlockSpec((eps, D_MODEL, D_EXPERT), lambda i, e: (e, 0, 0)),
                  pl.BlockSpec((eps, D_MODEL, D_EXPERT), lambda i, e: (e, 0, 0)),
                  pl.BlockSpec((eps, D_EXPERT, D_MODEL), lambda i, e: (e, 0, 0)),
                  full(g3), full(b3)],
        out_specs=row,
        out_shape=jax.ShapeDtypeStruct((n, D_MODEL), F32),
        scratch_shapes=[pltpu.VMEM((tm, D_MODEL), BF16), pltpu.VMEM((tm, LANES), F32),
                        pltpu.VMEM((tm, D_MODEL), F32)],
        compiler_params=_cparams(("arbitrary", "arbitrary")),
        name="moe_ln3",
    )(x, wr_t, e_bias, wsg, wsu, wsd, w_e_gate, w_e_up, w_e_down, g3, b3)


def _block_ones(seg):
    idx = jnp.arange(LANES, dtype=jnp.int32) // seg
    return (idx[:, None] == idx[None, :]).astype(BF16)


def _row(v):
    return v.reshape(1, -1).astype(F32)


def kernel(x_prompt, x_sample, cache_k, cache_v, state_wkv, state_shift, cache_mem_k, cache_mem_v, page_table, mem_prompt, ln_in_g, ln_in_b, rel_bias, w_in, mu_shift, w0, w_decay_up, a0, w_aaa_up, w_gate_up, k_k, k_a, r_k, lnx_g, lnx_b, lam_q1, lam_k1, lam_q2, lam_k2, subln_g, w_out, ln1_g, ln1_b, w_mq, w_mk, w_mv, w_mo, ln2_g, ln2_b, w_router, e_bias, w_e_gate, w_e_up, w_e_down, w_s_gate, w_s_up, w_s_down, ln3_g, ln3_b):
    assert w_in.shape[0] == DEPTH == 1
    nb, t, _ = x_prompt.shape
    db, s_len, _ = x_sample.shape
    l = 0
    seg64 = _block_ones(RWKV_HD)
    ones128 = _block_ones(LANES)
    zeros = jnp.zeros((64, 512), F32)
    w_lo = jnp.concatenate([jnp.concatenate([w_decay_up[l], zeros], axis=1),
                            jnp.concatenate([zeros, w_aaa_up[l]], axis=1)], axis=0).astype(BF16)
    prep_consts = (_row(mu_shift[l]), w_lo, _row(jnp.concatenate([w0[l], a0[l]])), w_gate_up[l].astype(BF16),
                   _row(k_k[l]), _row(k_a[l]), _row(r_k[l]), seg64)
    mix_consts = (_row(lnx_g[l]), _row(lnx_b[l]), _row(jnp.tile(subln_g[l] * (1.0 - LAM_INIT), N_DIFF_HEADS)),
                  w_out[l].astype(BF16), _row(ln1_g[l]), _row(ln1_b[l]), seg64, ones128)
    lam = (jnp.exp(jnp.sum(lam_q1[l] * lam_k1[l]).astype(F32)) - jnp.exp(jnp.sum(lam_q2[l] * lam_k2[l]).astype(F32))
           + LAM_INIT).reshape(1, 1)
    w_in_b = w_in[l].astype(BF16)
    wkt_b = jnp.transpose(w_in_b[:, RWKV_COLS + 512:RWKV_COLS + 1024])
    w_mq_b, w_mk_b, w_mv_b, w_mo_b = (w[l].astype(BF16) for w in (w_mq, w_mk, w_mv, w_mo))
    wr_t = jnp.transpose(w_router[l])
    eb = e_bias[l].reshape(N_EXPERTS, 1).astype(F32)
    wsg, wsu, wsd = w_s_gate[l].astype(BF16), w_s_up[l].astype(BF16), w_s_down[l].astype(BF16)
    g_in, b_in = _row(ln_in_g), _row(ln_in_b)
    g2, b2, g3, b3 = _row(ln2_g[l]), _row(ln2_b[l]), _row(ln3_g[l]), _row(ln3_b[l])

    def rwkv(p, prev, nseq, tlen, s0, tm, shift_in_kernel, tc):
        a, w, wr, b, k2, v, br, kr, g, bv = rwkv_prep(p, prev, prep_consts, tm, shift_in_kernel, tlen)
        y, st = wkv_scan(*[_to_scan_j(z, nseq, tlen) for z in (a, w, wr, b, k2)], _to_scan_i(v, nseq, tlen),
                         _to_scan_s(br, nseq, tlen), _to_scan_s(kr, nseq, tlen), s0, tc)
        return _from_scan_y(y, nseq, tlen), bv, g, _state_from_scan(st, nseq)

    def tail(h1, qm, mk, mv, nseq, tpad, tm, tm_moe):
        n = h1.shape[0]
        om = mem_attn(qm, mk, mv, min(tm, tpad))[:, :n // nseq].reshape(n, D_MODEL)
        h2 = matmul_res_ln(om, h1, w_mo_b, g2, b2, tm, "mem_out_ln2")
        return moe_ln(h2, wr_t, eb, wsg, wsu, wsd, w_e_gate[l], w_e_up[l], w_e_down[l], g3, b3, tm_moe)

    n_p = nb * t
    tm = 512
    hp, pp, qp, kp, vp = ln_proj(x_prompt.reshape(n_p, D_MODEL), g_in, b_in, w_in_b, wkt_b, tm, seq_len=t)
    s0_p = jnp.zeros((nb * N_RWKV_HEADS // SCAN_BH, SCAN_JLO, RWKV_HD, LANES), F32)
    y_p, bv_p, gt_p, wkv_p = rwkv(pp, None, nb, t, s0_p, tm, True, 128)
    od_p = diff_attn_prompt(qp, kp, vp, rel_bias, lam, nb, t, 512)
    h1_p = mix_ln(y_p, bv_p, gt_p, od_p, hp, mix_consts, tm)
    mem_rows = mem_prompt.reshape(nb * N_MEM, D_MODEL)
    mk_p = matmul(mem_rows, w_mk_b, tm, "mem_k")
    mv_p = matmul(mem_rows, w_mv_b, tm, "mem_v")
    qm_p = matmul(h1_p, w_mq_b, tm, "mem_q").reshape(nb, t, D_MODEL)
    h3_p = tail(h1_p, qm_p, mk_p.reshape(nb, N_MEM, D_MODEL), mv_p.reshape(nb, N_MEM, D_MODEL), nb, t, tm, 1024)

    n_s = db * s_len
    hs, ps, qs, ks, vs = ln_proj(x_sample.reshape(n_s, D_MODEL), g_in, b_in, w_in_b, wkt_b, n_s)
    ps3 = ps.reshape(db, s_len, RWKV_COLS)
    prev_s = jnp.concatenate([state_shift[l][:, None, :], ps3[:, :-1]], axis=1).reshape(n_s, RWKV_COLS)
    y_s, bv_s, gt_s, wkv_s = rwkv(ps, prev_s, db, s_len, _state_to_scan(state_wkv[l], db), n_s, False, s_len)
    n_pool = cache_k.shape[1]
    od_s = diff_attn_sample(qs.reshape(db, s_len, 512), ks.reshape(db, s_len, 512), vs.reshape(db, s_len, 512),
                            jnp.transpose(cache_k[l], (0, 2, 3, 4, 1)).reshape(n_pool, DIFF_WIDTH, PAGE_SIZE),
                            cache_v[l].reshape(n_pool, PAGE_SIZE * N_DIFF_HEADS, DIFF_VD),
                            page_table, rel_bias, lam, 16)
    h1_s = mix_ln(y_s, bv_s, gt_s, od_s.reshape(n_s, DIFF_WIDTH), hs, mix_consts, n_s)
    qm_s = matmul(h1_s, w_mq_b, n_s, "mem_q").reshape(db, s_len, D_MODEL)
    qm_s = jnp.pad(qm_s, ((0, 0), (0, SUBLANES - s_len), (0, 0)))
    h3_s = tail(h1_s, qm_s, cache_mem_k[l].reshape(db, N_MEM, D_MODEL), cache_mem_v[l].reshape(db, N_MEM, D_MODEL),
                db, SUBLANES, n_s, n_s)

    return (h3_p.reshape(nb, t, D_MODEL), h3_s.reshape(db, s_len, D_MODEL),
            jnp.transpose(kp.reshape(1, nb, N_DIFF_HEADS, 2, DIFF_HD, t), (0, 1, 5, 2, 3, 4)),
            vp.reshape(1, nb, t, N_DIFF_HEADS, DIFF_VD),
            wkv_p[None], pp.reshape(nb, t, RWKV_COLS)[:, -1][None],
            mk_p.reshape(1, nb, N_MEM, N_MEM_HEADS, MEM_HD), mv_p.reshape(1, nb, N_MEM, N_MEM_HEADS, MEM_HD),
            ks.reshape(1, db, s_len, N_DIFF_HEADS, 2, DIFF_HD), vs.reshape(1, db, s_len, N_DIFF_HEADS, DIFF_VD),
            wkv_s[None], ps3[:, -1][None])
```

```python
import functools
import math

import jax
import jax.numpy as jnp
from jax import lax
from jax.experimental import pallas as pl
from jax.experimental.pallas import tpu as pltpu

F32 = jnp.float32
BF16 = jnp.bfloat16

D_MODEL = 1024
RWKV_WIDTH = 512
RWKV_HD = 64
N_RWKV_HEADS = 8
RWKV_COLS = 1792
GN_EPS = 64e-5
DIFF_WIDTH = 512
DIFF_HD = 64
DIFF_VD = 128
N_DIFF_HEADS = 4
RMS_EPS = 1e-5
N_BUCKETS = 32
MAX_DISTANCE = 128
PAGE_SIZE = 128
N_MEM = 256
N_MEM_HEADS = 4
MEM_HD = 256
N_EXPERTS = 64
N_GROUPS = 8
GROUP_SIZE = 8
TOPK_GROUPS = 4
TOP_K = 8
D_EXPERT = 256
ROUTED_SCALE = 2.5
MOE_EXPERTS_PER_STEP = 4
DEPTH = 1
DN_ALPHA = (2.0 * DEPTH) ** 0.25
LN_EPS = 1e-5
NEG_INF = -1e30
LAM_INIT = 0.8 - 0.6 * math.exp(-0.3 * 0)

LANES = 128
SUBLANES = 8
VMEM_LIMIT = 60 * 1024 * 1024

SCAN_JLO = 16
SCAN_JHI = RWKV_HD // SCAN_JLO
SCAN_BH = LANES // SCAN_JHI


def _cparams(sem):
    return pltpu.CompilerParams(dimension_semantics=sem, vmem_limit_bytes=VMEM_LIMIT)


def _layer_norm(x, g, b):
    mu = jnp.mean(x, -1, keepdims=True)
    d = x - mu
    var = jnp.mean(d * d, -1, keepdims=True)
    return d * lax.rsqrt(var + LN_EPS) * g + b


def _dot(a, b):
    return jnp.dot(a, b, preferred_element_type=F32)


def _dot_nt(a, b):
    return lax.dot_general(a, b, (((1,), (1,)), ((), ())), preferred_element_type=F32)


def _seg_sum(x, ones_blk):
    outs = []
    for c in range(x.shape[1] // LANES):
        xc = x[:, c * LANES:(c + 1) * LANES]
        hi = xc.astype(BF16)
        lo = (xc - hi.astype(F32)).astype(BF16)
        outs.append(_dot(hi, ones_blk) + _dot(lo, ones_blk))
    return outs[0] if len(outs) == 1 else jnp.concatenate(outs, axis=1)


def _ln_proj_body(k_transposed, x_ref, g_ref, b_ref, w_ref, wkt_ref, h_ref, p_ref, q_ref, k_ref, v_ref):
    h = _layer_norm(x_ref[...], g_ref[...], b_ref[...])
    h_ref[...] = h
    hb = h.astype(BF16)
    c0 = RWKV_COLS
    p_ref[...] = _dot(hb, w_ref[:, 0:c0])
    q_ref[...] = _dot(hb, w_ref[:, c0:c0 + 512])
    if k_transposed:
        k_ref[0] = _dot_nt(wkt_ref[...], hb)
    else:
        k_ref[...] = _dot(hb, w_ref[:, c0 + 512:c0 + 1024])
    v_ref[...] = _dot(hb, w_ref[:, c0 + 1024:c0 + 1536])


def ln_proj(x, g, b, w_bf16, wkt_bf16, tm, seq_len=None):
    n = x.shape[0]
    row = lambda width: pl.BlockSpec((tm, width), lambda i: (i, 0))
    full = lambda a: pl.BlockSpec(a.shape, lambda i: (0,) * a.ndim)
    if seq_len is None:
        k_spec, k_shape = row(512), (n, 512)
    else:
        bps = seq_len // tm
        k_spec = pl.BlockSpec((1, 512, tm), lambda i: (i // bps, 0, i % bps))
        k_shape = (n // seq_len, 512, seq_len)
    shapes = [(n, D_MODEL), (n, RWKV_COLS), (n, 512), k_shape, (n, 512)]
    return pl.pallas_call(
        functools.partial(_ln_proj_body, seq_len is not None),
        grid=(n // tm,),
        in_specs=[row(D_MODEL), full(g), full(b), full(w_bf16), full(wkt_bf16)],
        out_specs=[row(D_MODEL), row(RWKV_COLS), row(512), k_spec, row(512)],
        out_shape=[jax.ShapeDtypeStruct(sh, F32) for sh in shapes],
        compiler_params=_cparams(("arbitrary",)),
        name="ln_proj",
    )(x, g, b, w_bf16, wkt_bf16)


def _mm_body(x_ref, w_ref, o_ref):
    o_ref[...] = _dot(x_ref[...].astype(BF16), w_ref[...])


def matmul(x, w_bf16, tm, name):
    n, kdim = x.shape
    nout = w_bf16.shape[1]
    return pl.pallas_call(
        _mm_body,
        grid=(n // tm,),
        in_specs=[pl.BlockSpec((tm, kdim), lambda i: (i, 0)), pl.BlockSpec((kdim, nout), lambda i: (0, 0))],
        out_specs=pl.BlockSpec((tm, nout), lambda i: (i, 0)),
        out_shape=jax.ShapeDtypeStruct((n, nout), F32),
        compiler_params=_cparams(("arbitrary",)),
        name=name,
    )(x, w_bf16)


def _mm_res_ln_body(x_ref, res_ref, w_ref, g_ref, b_ref, o_ref):
    y = _dot(x_ref[...].astype(BF16), w_ref[...])
    o_ref[...] = _layer_norm(DN_ALPHA * res_ref[...] + y, g_ref[...], b_ref[...])


def matmul_res_ln(x, res, w_bf16, g, b, tm, name):
    n, kdim = x.shape
    nout = w_bf16.shape[1]
    row = lambda width: pl.BlockSpec((tm, width), lambda i: (i, 0))
    full = lambda a: pl.BlockSpec(a.shape, lambda i: (0,) * a.ndim)
    return pl.pallas_call(
        _mm_res_ln_body,
        grid=(n // tm,),
        in_specs=[row(kdim), row(nout), full(w_bf16), full(g), full(b)],
        out_specs=row(nout),
        out_shape=jax.ShapeDtypeStruct((n, nout), F32),
        compiler_params=_cparams(("arbitrary",)),
        name=name,
    )(x, res, w_bf16, g, b)


def _rwkv_prep_body(shift_in_kernel, seq_blocks, p_ref, prev_ref, mu_ref, wlo_ref, w0a0_ref, wg_ref,
                    kk_ref, ka_ref, rk_ref, seg_ref,
                    a_o, w_o, wr_o, b_o, k_o, v_o, br_o, kr_o, g_o, bv_o):
    p = p_ref[...]
    if shift_in_kernel:
        first = jnp.where(pl.program_id(0) % seq_blocks == 0, 0.0, prev_ref[SUBLANES - 1:SUBLANES, :])
        row = lax.broadcasted_iota(jnp.int32, p.shape, 0)
        p_prev = jnp.where(row == 0, first, pltpu.roll(p, 1, 0))
    else:
        p_prev = prev_ref[...]
    ps = p + (p_prev - p) * mu_ref[...]
    r = ps[:, 0:512]
    k = ps[:, 512:1024]
    v = ps[:, 1024:1536]
    lo = ps[:, 1536:1664]
    g_lo = ps[:, 1664:1792]
    lane = lax.broadcasted_iota(jnp.int32, lo.shape, 1)
    lo = jnp.where(lane < 64, jnp.tanh(lo), lo)
    wa = _dot(lo.astype(BF16), wlo_ref[...]) + w0a0_ref[...]
    x = -wa[:, 0:512]
    softplus = jnp.maximum(x, 0.0) + jnp.log(1.0 + jnp.exp(-jnp.abs(x)))
    decay = jnp.exp(-jnp.exp(-softplus - 0.5))
    a = jax.nn.sigmoid(wa[:, 512:1024])
    g = _dot(jax.nn.sigmoid(g_lo).astype(BF16), wg_ref[...])
    seg = seg_ref[...]
    kk = k * kk_ref[...]
    kk = kk / jnp.maximum(jnp.sqrt(_seg_sum(kk * kk, seg)), 1e-12)
    k2 = k * (1.0 + (a - 1.0) * ka_ref[...])
    b = kk * a
    a_o[...] = -kk
    w_o[...] = decay
    wr_o[...] = decay * r
    b_o[...] = b
    k_o[...] = k2
    v_o[...] = v
    br_o[...] = _seg_sum(b * r, seg)
    kr_o[...] = _seg_sum(k2 * r, seg)
    g_o[...] = g
    bv_o[...] = _seg_sum(r * k2 * rk_ref[...], seg) * v


def rwkv_prep(p, prev, consts, tm, shift_in_kernel, seq_len):
    n = p.shape[0]
    row = lambda width: pl.BlockSpec((tm, width), lambda i: (i, 0))
    full = lambda a: pl.BlockSpec(a.shape, lambda i: (0,) * a.ndim)
    if shift_in_kernel:
        per8 = tm // SUBLANES
        prev_spec = pl.BlockSpec((SUBLANES, RWKV_COLS), lambda i: (jnp.maximum(i * per8 - 1, 0), 0))
        prev = p
    else:
        prev_spec = row(RWKV_COLS)
    body = functools.partial(_rwkv_prep_body, shift_in_kernel, max(seq_len // tm, 1))
    return pl.pallas_call(
        body,
        grid=(n // tm,),
        in_specs=[row(RWKV_COLS), prev_spec] + [full(c) for c in consts],
        out_specs=[row(512)] * 10,
        out_shape=[jax.ShapeDtypeStruct((n, 512), F32)] * 10,
        compiler_params=_cparams(("arbitrary",)),
        name="rwkv_prep",
    )(p, prev, *consts)


def _lane_fold(x):
    return x + pltpu.roll(x, 32, 1) + pltpu.roll(x, 64, 1) + pltpu.roll(x, 96, 1)


def _wkv_scan_body(tc, a_ref, w_ref, wr_ref, b_ref, k_ref, v_ref, br_ref, kr_ref, s0_ref,
                   y_ref, st_ref, s_scr):
    c = pl.program_id(1)
    ngrp = RWKV_HD // SUBLANES

    @pl.when(c == 0)
    def _():
        s_scr[...] = s0_ref[0]

    def bcast(ref, t, j):
        return jnp.broadcast_to(ref[0, t, pl.ds(j, 1), :], (SUBLANES, LANES))

    lane_grp = lax.broadcasted_iota(jnp.int32, (SUBLANES, LANES), 1) // SCAN_BH

    def step(t, carry):
        acc_a = [jnp.zeros((SUBLANES, LANES), F32) for _ in range(ngrp)]
        acc_y = [jnp.zeros((SUBLANES, LANES), F32) for _ in range(ngrp)]
        for j in range(SCAN_JLO):
            ab = bcast(a_ref, t, j)
            wrb = bcast(wr_ref, t, j)
            for ig in range(ngrp):
                s = s_scr[j, ig * SUBLANES:(ig + 1) * SUBLANES, :]
                acc_a[ig] = acc_a[ig] + s * ab
                acc_y[ig] = acc_y[ig] + s * wrb
        sa = [_lane_fold(x) for x in acc_a]
        vp = [v_ref[0, t, m * SUBLANES:(m + 1) * SUBLANES, :] for m in range(2)]
        vv = [_lane_fold(jnp.where(lane_grp == ig // 2, vp[ig % 2], 0.0)) for ig in range(ngrp)]
        for j in range(SCAN_JLO):
            wb = bcast(w_ref, t, j)
            bb = bcast(b_ref, t, j)
            kb = bcast(k_ref, t, j)
            for ig in range(ngrp):
                sl = slice(ig * SUBLANES, (ig + 1) * SUBLANES)
                s_scr[j, sl, :] = s_scr[j, sl, :] * wb + sa[ig] * bb + vv[ig] * kb
        br = jnp.broadcast_to(br_ref[0, t], (SUBLANES, LANES))
        kr = jnp.broadcast_to(kr_ref[0, t], (SUBLANES, LANES))
        y = [_lane_fold(acc_y[ig]) + sa[ig] * br + vv[ig] * kr for ig in range(ngrp)]
        for m in range(2):
            yp = y[6 + m]
            for q in (2, 1, 0):
                yp = jnp.where(lane_grp == q, y[2 * q + m], yp)
            y_ref[0, t, m * SUBLANES:(m + 1) * SUBLANES, :] = yp
        return carry

    lax.fori_loop(0, tc, step, 0)

    @pl.when(c == pl.num_programs(1) - 1)
    def _():
        st_ref[0] = s_scr[...]


def wkv_scan(a, w, wr, b, k, v, br, kr, s0, tc):
    g, t = a.shape[0], a.shape[1]
    jspec = pl.BlockSpec((1, tc, SCAN_JLO, LANES), lambda gi, c: (gi, c, 0, 0))
    sspec = pl.BlockSpec((1, tc, 1, LANES), lambda gi, c: (gi, c, 0, 0))
    stspec = pl.BlockSpec((1, SCAN_JLO, RWKV_HD, LANES), lambda gi, c: (gi, 0, 0, 0))
    return pl.pallas_call(
        functools.partial(_wkv_scan_body, tc),
        grid=(g, t // tc),
        in_specs=[jspec] * 6 + [sspec, sspec, stspec],
        out_specs=[jspec, stspec],
        out_shape=[jax.ShapeDtypeStruct((g, t, SCAN_JLO, LANES), F32),
                   jax.ShapeDtypeStruct((g, SCAN_JLO, RWKV_HD, LANES), F32)],
        scratch_shapes=[pltpu.VMEM((SCAN_JLO, RWKV_HD, LANES), F32)],
        compiler_params=_cparams(("arbitrary", "arbitrary")),
        name="wkv_scan",
    )(a, w, wr, b, k, v, br, kr, s0)


def _to_scan_j(x, nb, t):
    g = nb * N_RWKV_HEADS // SCAN_BH
    bp = nb // g
    x = x.reshape(g, bp, t, N_RWKV_HEADS, SCAN_JHI, SCAN_JLO)
    return jnp.transpose(x, (0, 2, 5, 4, 1, 3)).reshape(g, t, SCAN_JLO, LANES)


def _from_scan_j(x, nb, t):
    g = x.shape[0]
    bp = nb // g
    x = x.reshape(g, t, SCAN_JLO, SCAN_JHI, bp, N_RWKV_HEADS)
    return jnp.transpose(x, (0, 4, 1, 5, 3, 2)).reshape(nb * t, RWKV_WIDTH)


def _to_scan_s(x, nb, t):
    g = nb * N_RWKV_HEADS // SCAN_BH
    bp = nb // g
    x = x.reshape(g, bp, t, N_RWKV_HEADS, RWKV_HD)[..., 0]
    x = jnp.transpose(x, (0, 2, 1, 3)).reshape(g, t, 1, 1, SCAN_BH)
    return jnp.broadcast_to(x, (g, t, 1, SCAN_JHI, SCAN_BH)).reshape(g, t, 1, LANES)


def _state_to_scan(s, nb):
    g = nb * N_RWKV_HEADS // SCAN_BH
    bp = nb // g
    s = s.reshape(g, bp, N_RWKV_HEADS, RWKV_HD, SCAN_JHI, SCAN_JLO)
    return jnp.transpose(s, (0, 5, 3, 4, 1, 2)).reshape(g, SCAN_JLO, RWKV_HD, LANES)


def _state_from_scan(s, nb):
    g = s.shape[0]
    bp = nb // g
    s = s.reshape(g, SCAN_JLO, RWKV_HD, SCAN_JHI, bp, N_RWKV_HEADS)
    return jnp.transpose(s, (0, 4, 5, 2, 3, 1)).reshape(nb, N_RWKV_HEADS, RWKV_HD, RWKV_HD)


def _mix_body(y_ref, bv_ref, g_ref, od_ref, h_ref, lnxg_ref, lnxb_ref, sub_ref, wout_ref, g1_ref, b1_ref,
              seg_ref, ones_ref, o_ref):
    seg = seg_ref[...]
    y = y_ref[...]
    mu = _seg_sum(y, seg) * (1.0 / RWKV_HD)
    d = y - mu
    var = _seg_sum(d * d, seg) * (1.0 / RWKV_HD)
    yn = d * lax.rsqrt(var + GN_EPS) * lnxg_ref[...] + lnxb_ref[...]
    o_rw = (yn + bv_ref[...]) * g_ref[...]
    od = od_ref[...]
    ms = _seg_sum(od * od, ones_ref[...]) * (1.0 / DIFF_VD)
    o_df = od * lax.rsqrt(ms + RMS_EPS) * sub_ref[...]
    mix = _dot(o_rw.astype(BF16), wout_ref[0:512, :]) + _dot(o_df.astype(BF16), wout_ref[512:1024, :])
    o_ref[...] = _layer_norm(DN_ALPHA * h_ref[...] + mix, g1_ref[...], b1_ref[...])


def mix_ln(y, bv, g, od, h, consts, tm):
    n = y.shape[0]
    row = lambda width: pl.BlockSpec((tm, width), lambda i: (i, 0))
    full = lambda a: pl.BlockSpec(a.shape, lambda i: (0,) * a.ndim)
    return pl.pallas_call(
        _mix_body,
        grid=(n // tm,),
        in_specs=[row(512)] * 4 + [row(D_MODEL)] + [full(c) for c in consts],
        out_specs=row(D_MODEL),
        out_shape=jax.ShapeDtypeStruct((n, D_MODEL), F32),
        compiler_params=_cparams(("arbitrary",)),
        name="mix_ln1",
    )(y, bv, g, od, h, *consts)


def _t5_bucket(dist):
    max_exact = N_BUCKETS // 2
    d = jnp.maximum(dist, 1).astype(F32)
    large = max_exact + (jnp.log(d / max_exact) / math.log(MAX_DISTANCE / max_exact)
                         * (N_BUCKETS - max_exact)).astype(jnp.int32)
    return jnp.where(dist < max_exact, dist, jnp.minimum(large, N_BUCKETS - 1))


def _bias_table(rel_bias, dist):
    bucket = _t5_bucket(jnp.maximum(dist, 0))
    tail = (1,) * dist.ndim
    bias = jnp.zeros(rel_bias.shape[1:] + dist.shape, F32)
    for bk in range(N_BUCKETS):
        bias = jnp.where(bucket == bk, rel_bias[bk].astype(F32).reshape(rel_bias.shape[1:] + tail), bias)
    return jnp.where(dist >= 0, bias, NEG_INF)


def _toeplitz_bias(rel_bias, base, tq, tk):
    n = tq + tk - 1
    g = _bias_table(rel_bias, base - (tk - 1) + jnp.arange(n, dtype=jnp.int32))
    h = g[..., ::-1]
    rows = jnp.tile(h, (1, 1, tq + 1))[..., :tq * (n + 1)].reshape(h.shape[:-1] + (tq, n + 1))
    return rows[..., ::-1, :tk]


def _dap_body(tq, tk, qi_ref, ki_ref, lam_ref, q_ref, k_ref, v_ref, bias_ref, cfar_ref, o_ref,
              qs_scr, m_scr, l_scr, acc_scr):
    step = pl.program_id(2)
    qi = qi_ref[step]
    ki = ki_ref[step]

    @pl.when(ki == 0)
    def _():
        q = q_ref[...] * (DIFF_HD ** -0.5)
        lane = lax.broadcasted_iota(jnp.int32, q.shape, 1)
        qs_scr[0:tq, :] = jnp.where(lane < DIFF_HD, q, 0.0).astype(BF16)
        qs_scr[tq:2 * tq, :] = jnp.where(lane >= DIFF_HD, q, 0.0).astype(BF16)
        m_scr[...] = jnp.full(m_scr.shape, NEG_INF, F32)
        l_scr[...] = jnp.zeros(l_scr.shape, F32)
        acc_scr[...] = jnp.zeros(acc_scr.shape, F32)

    def update(s, shift):
        m_prev = m_scr[...]
        m_new = jnp.maximum(m_prev, jnp.max(s, axis=1, keepdims=True) + shift)
        alpha = jnp.exp(m_prev - m_new)
        p = jnp.exp(s - jnp.tile(m_new - shift, (1, tk // LANES)))
        l_scr[...] = alpha * l_scr[...] + jnp.sum(p, axis=1, keepdims=True)
        acc_scr[...] = alpha * acc_scr[...] + _dot(p.astype(BF16), v_ref[...].astype(BF16))
        m_scr[...] = m_new

    @pl.when(ki < qi - 1)
    def _():
        update(_dot(qs_scr[...], k_ref[0].astype(BF16)), cfar_ref[0])

    @pl.when(ki >= qi - 1)
    def _():
        update(_dot(qs_scr[...], k_ref[0].astype(BF16)) + bias_ref[0, 0], jnp.zeros((2 * tq, LANES), F32))

    @pl.when(ki == qi)
    def _():
        o = acc_scr[...] / l_scr[...]
        o_ref[...] = o[0:tq] - lam_ref[0, 0] * o[tq:2 * tq]


def diff_attn_prompt(q, kt, v, rel_bias, lam, nb, t, tq):
    tk = tq
    assert tq >= MAX_DISTANCE and t % tq == 0
    nq = t // tq
    qi_list, ki_list = [], []
    for a in range(nq):
        for c in range(a + 1):
            qi_list.append(a)
            ki_list.append(c)
    qi_arr = jnp.asarray(qi_list, jnp.int32)
    ki_arr = jnp.asarray(ki_list, jnp.int32)
    tiles = jnp.stack([_toeplitz_bias(rel_bias, tq, tq, tk), _toeplitz_bias(rel_bias, 0, tq, tk)], axis=1)
    tiles = tiles.reshape(N_DIFF_HEADS, 2, 2 * tq, tk)
    cfar = _bias_table(rel_bias, jnp.full((tq, LANES), 2 * MAX_DISTANCE, jnp.int32))
    cfar = cfar.reshape(N_DIFF_HEADS, 2 * tq, LANES)

    grid_spec = pltpu.PrefetchScalarGridSpec(
        num_scalar_prefetch=2,
        grid=(nb, N_DIFF_HEADS, len(qi_list)),
        in_specs=[
            pl.BlockSpec(memory_space=pltpu.SMEM),
            pl.BlockSpec((tq, LANES), lambda b, h, s, qi, ki: (b * nq + qi[s], h)),
            pl.BlockSpec((1, LANES, tk), lambda b, h, s, qi, ki: (b, h, ki[s])),
            pl.BlockSpec((tk, LANES), lambda b, h, s, qi, ki: (b * nq + ki[s], h)),
            pl.BlockSpec((1, 1, 2 * tq, tk), lambda b, h, s, qi, ki: (h, jnp.where(ki[s] == qi[s], 1, 0), 0, 0)),
            pl.BlockSpec((1, 2 * tq, LANES), lambda b, h, s, qi, ki: (h, 0, 0)),
        ],
        out_specs=pl.BlockSpec((tq, LANES), lambda b, h, s, qi, ki: (b * nq + qi[s], h)),
        scratch_shapes=[pltpu.VMEM((2 * tq, LANES), BF16), pltpu.VMEM((2 * tq, LANES), F32),
                        pltpu.VMEM((2 * tq, LANES), F32), pltpu.VMEM((2 * tq, LANES), F32)],
    )
    return pl.pallas_call(
        functools.partial(_dap_body, tq, tk),
        grid_spec=grid_spec,
        out_shape=jax.ShapeDtypeStruct((nb * t, DIFF_WIDTH), F32),
        compiler_params=_cparams(("arbitrary", "arbitrary", "arbitrary")),
        name="diff_attn_prompt",
    )(qi_arr, ki_arr, lam, q, kt, v, tiles, cfar)


def _das_body(npg, s_len, pt_ref, lam_ref, q_ref, qmask_ref, bias_ref, bnew_ref, kn_ref, vn_ref, *rest):
    k_refs = rest[0:npg]
    v_refs = rest[npg:2 * npg]
    o_ref = rest[2 * npg]
    qs_scr, m_scr, l_scr, acc_scr = rest[2 * npg + 1:]
    g = pl.program_id(1)
    rows = 2 * N_DIFF_HEADS * s_len

    @pl.when(g == 0)
    def _():
        q = q_ref[0] * (DIFF_HD ** -0.5)
        qrep = jnp.concatenate([q] * (2 * N_DIFF_HEADS), axis=0)
        qs_scr[...] = (qrep * qmask_ref[...]).astype(BF16)
        m_scr[...] = jnp.full(m_scr.shape, NEG_INF, F32)
        l_scr[...] = jnp.zeros(l_scr.shape, F32)
        acc_scr[...] = jnp.zeros(acc_scr.shape, F32)

    hrows = 2 * s_len
    qs = qs_scr[...]
    s = jnp.concatenate([_dot(qs, kr[0].astype(BF16)) for kr in k_refs], axis=1) + bias_ref[0]
    m_prev = m_scr[...]
    m_new = jnp.maximum(m_prev, jnp.max(s, axis=1, keepdims=True))
    alpha = jnp.exp(m_prev - m_new)
    p = jnp.exp(s - jnp.tile(m_new, (1, npg)))
    l_scr[...] = alpha * l_scr[...] + jnp.sum(p, axis=1, keepdims=True)
    pvs = []
    for h in range(N_DIFF_HEADS):
        pv = None
        for j in range(npg):
            ph = p[h * hrows:(h + 1) * hrows, j * PAGE_SIZE:(j + 1) * PAGE_SIZE].astype(BF16)
            vh = v_refs[j][0, pl.ds(h, PAGE_SIZE, stride=N_DIFF_HEADS), :].astype(BF16)
            d = _dot(ph, vh)
            pv = d if pv is None else pv + d
        pvs.append(pv)
    acc_scr[...] = alpha * acc_scr[...] + jnp.concatenate(pvs, axis=0)
    m_scr[...] = m_new

    @pl.when(g == pl.num_programs(1) - 1)
    def _():
        qf = qs_scr[...].astype(F32)
        kn = kn_ref[0].astype(BF16).astype(F32)
        vn = vn_ref[0].astype(BF16).astype(F32)
        lane = lax.broadcasted_iota(jnp.int32, (rows, LANES), 1)
        sn = bnew_ref[...]
        for j in range(s_len):
            dj = jnp.sum(qf * kn[j:j + 1, :], axis=1, keepdims=True)
            sn = sn + jnp.where(lane == j, dj, 0.0)
        m_prev = m_scr[...]
        m_new = jnp.maximum(m_prev, jnp.max(sn, axis=1, keepdims=True))
        alpha = jnp.exp(m_prev - m_new)
        pn = jnp.exp(sn - m_new).astype(BF16).astype(F32)
        l_fin = alpha * l_scr[...] + jnp.sum(pn, axis=1, keepdims=True)
        acc = alpha * acc_scr[...]
        for j in range(s_len):
            vj = jnp.concatenate([jnp.broadcast_to(vn[j:j + 1, h * DIFF_VD:(h + 1) * DIFF_VD], (hrows, DIFF_VD))
                                  for h in range(N_DIFF_HEADS)], axis=0)
            acc = acc + pn[:, j:j + 1] * vj
        o = acc / l_fin
        lam = lam_ref[0, 0]
        outs = []
        for h in range(N_DIFF_HEADS):
            blk = o[h * hrows:(h + 1) * hrows, :]
            outs.append(blk[0:s_len] - lam * blk[s_len:hrows])
        o_ref[0] = jnp.concatenate(outs, axis=1)


def diff_attn_sample(q, k, v, cache_k, cache_v, page_table, rel_bias, lam, npg):
    db, s_len, _ = q.shape
    n_pages = page_table.shape[1]
    past = n_pages * PAGE_SIZE
    assert n_pages % npg == 0
    ngroups = n_pages // npg
    rows = 2 * N_DIFF_HEADS * s_len
    width = npg * PAGE_SIZE
    hm = jnp.arange(rows, dtype=jnp.int32) // s_len
    qmask = (jnp.arange(DIFF_WIDTH, dtype=jnp.int32)[None, :] // DIFF_HD == hm[:, None]).astype(F32)
    qpos = past + jnp.arange(s_len, dtype=jnp.int32)

    def rows_of(table):
        return table.reshape(rows, table.shape[-1])

    far = rows_of(_bias_table(rel_bias, jnp.full((s_len, width), 2 * MAX_DISTANCE, jnp.int32)))
    kpos_last = past - width + jnp.arange(width, dtype=jnp.int32)
    last = rows_of(_bias_table(rel_bias, qpos[:, None] - kpos_last[None, :]))
    bias = jnp.stack([far, last], axis=0)
    knew_pos = past + jnp.arange(LANES, dtype=jnp.int32)
    dist_new = jnp.where(jnp.arange(LANES)[None, :] < s_len, qpos[:, None] - knew_pos[None, :], -1)
    bnew = rows_of(_bias_table(rel_bias, dist_new))
    assert width >= MAX_DISTANCE + s_len
    pt_flat = page_table.reshape(-1).astype(jnp.int32)

    def page_spec(j):
        return pl.BlockSpec((1, DIFF_WIDTH, PAGE_SIZE),
                            lambda b, g, pt, j=j: (pt[b * n_pages + g * npg + j], 0, 0))

    seq_spec = pl.BlockSpec((1, s_len, DIFF_WIDTH), lambda b, g, pt: (b, 0, 0))
    full2 = lambda a: pl.BlockSpec(a.shape, lambda b, g, pt: (0,) * a.ndim)
    grid_spec = pltpu.PrefetchScalarGridSpec(
        num_scalar_prefetch=1,
        grid=(db, ngroups),
        in_specs=[pl.BlockSpec(memory_space=pltpu.SMEM), seq_spec, full2(qmask),
                  pl.BlockSpec((1, rows, width), lambda b, g, pt: (jnp.where(g == ngroups - 1, 1, 0), 0, 0)),
                  full2(bnew), seq_spec, seq_spec]
                 + [page_spec(j) for j in range(npg)] + [page_spec(j) for j in range(npg)],
        out_specs=seq_spec,
        scratch_shapes=[pltpu.VMEM((rows, DIFF_WIDTH), BF16), pltpu.VMEM((rows, LANES), F32),
                        pltpu.VMEM((rows, LANES), F32), pltpu.VMEM((rows, DIFF_VD), F32)],
    )
    return pl.pallas_call(
        functools.partial(_das_body, npg, s_len),
        grid_spec=grid_spec,
        out_shape=jax.ShapeDtypeStruct((db, s_len, DIFF_WIDTH), F32),
        compiler_params=_cparams(("arbitrary", "arbitrary")),
        name="diff_attn_sample",
    )(pt_flat, lam, q, qmask, bias, bnew, k, v, *([cache_k] * npg), *([cache_v] * npg))


def _mem_attn_body(q_ref, mk_ref, mv_ref, o_ref):
    for h in range(N_MEM_HEADS):
        cols = slice(h * MEM_HD, (h + 1) * MEM_HD)
        qh = (q_ref[0, :, cols] * (MEM_HD ** -0.5)).astype(BF16)
        s = _dot_nt(qh, mk_ref[0, :, cols].astype(BF16))
        p = jnp.exp(s - jnp.max(s, axis=1, keepdims=True))
        l = jnp.sum(p, axis=1, keepdims=True)
        o_ref[0, :, cols] = _dot(p.astype(BF16), mv_ref[0, :, cols].astype(BF16)) / l


def mem_attn(q, mk, mv, tm):
    nb, t, _ = q.shape
    qspec = pl.BlockSpec((1, tm, D_MODEL), lambda b, i: (b, i, 0))
    mspec = pl.BlockSpec((1, N_MEM, D_MODEL), lambda b, i: (b, 0, 0))
    return pl.pallas_call(
        _mem_attn_body,
        grid=(nb, t // tm),
        in_specs=[qspec, mspec, mspec],
        out_specs=qspec,
        out_shape=jax.ShapeDtypeStruct((nb, t, D_MODEL), F32),
        compiler_params=_cparams(("arbitrary", "arbitrary")),
        name="mem_attn",
    )(q, mk, mv)


MEM_COL_TILES = MEM_HD // LANES


def _mem_cache_rows(cache):
    nb = cache.shape[0]
    c = cache.reshape(nb, N_MEM, N_MEM_HEADS, MEM_COL_TILES, LANES)
    return jnp.transpose(c, (0, 1, 3, 2, 4)).reshape(nb, N_MEM * MEM_COL_TILES * N_MEM_HEADS, LANES)


def _mem_attn_cached_body(q_ref, mk_ref, mv_ref, o_ref):
    stride = MEM_COL_TILES * N_MEM_HEADS

    def tile(ref, h, ct):
        return ref[0, pl.ds(ct * N_MEM_HEADS + h, N_MEM, stride=stride), :].astype(BF16)

    for h in range(N_MEM_HEADS):
        s = None
        for ct in range(MEM_COL_TILES):
            c0 = h * MEM_HD + ct * LANES
            qh = (q_ref[0, :, c0:c0 + LANES] * (MEM_HD ** -0.5)).astype(BF16)
            d = _dot_nt(qh, tile(mk_ref, h, ct))
            s = d if s is None else s + d
        p = jnp.exp(s - jnp.max(s, axis=1, keepdims=True))
        l = jnp.sum(p, axis=1, keepdims=True)
        pb = p.astype(BF16)
        for ct in range(MEM_COL_TILES):
            c0 = h * MEM_HD + ct * LANES
            o_ref[0, :, c0:c0 + LANES] = _dot(pb, tile(mv_ref, h, ct)) / l


def mem_attn_cached(q, mk_rows, mv_rows):
    nb, t, _ = q.shape
    qspec = pl.BlockSpec((1, t, D_MODEL), lambda b: (b, 0, 0))
    mspec = pl.BlockSpec((1,) + mk_rows.shape[1:], lambda b: (b, 0, 0))
    return pl.pallas_call(
        _mem_attn_cached_body,
        grid=(nb,),
        in_specs=[qspec, mspec, mspec],
        out_specs=qspec,
        out_shape=jax.ShapeDtypeStruct((nb, t, D_MODEL), F32),
        compiler_params=_cparams(("arbitrary",)),
        name="mem_attn_cached",
    )(q, mk_rows, mv_rows)


def _route(x, wr_t, e_bias):
    tm = x.shape[0]
    logits = lax.dot_general(wr_t, x, (((1,), (1,)), ((), ())), precision=lax.Precision.HIGHEST,
                             preferred_element_type=F32)
    scores = jax.nn.sigmoid(logits)
    biased = scores + e_bias
    sub = lax.broadcasted_iota(jnp.int32, (GROUP_SIZE, tm), 0)
    grp_rows = []
    for gi in range(N_GROUPS):
        xg = biased[gi * GROUP_SIZE:(gi + 1) * GROUP_SIZE, :]
        m1 = jnp.max(xg, axis=0, keepdims=True)
        i1 = jnp.min(jnp.where(xg == m1, sub, GROUP_SIZE), axis=0, keepdims=True)
        m2 = jnp.max(jnp.where(sub == i1, -jnp.inf, xg), axis=0, keepdims=True)
        grp_rows.append(m1 + m2)
    grp = jnp.concatenate(grp_rows, axis=0)
    gidx = lax.broadcasted_iota(jnp.int32, (N_GROUPS, tm), 0)
    grank = jnp.zeros((N_GROUPS, tm), jnp.int32)
    for gi in range(N_GROUPS):
        other = jnp.broadcast_to(grp[gi:gi + 1, :], (N_GROUPS, tm))
        grank = grank + ((other > grp) | ((other == grp) & (gi < gidx))).astype(jnp.int32)
    gsel = grank < TOPK_GROUPS
    emask = jnp.concatenate([jnp.broadcast_to(gsel[gi:gi + 1, :], (GROUP_SIZE, tm)) for gi in range(N_GROUPS)],
                            axis=0)
    masked = jnp.where(emask, biased, NEG_INF)
    eidx = lax.broadcasted_iota(jnp.int32, (N_EXPERTS, tm), 0)
    erank = jnp.zeros((N_EXPERTS, tm), jnp.int32)
    for e in range(N_EXPERTS):
        other = jnp.broadcast_to(masked[e:e + 1, :], (N_EXPERTS, tm))
        erank = erank + ((other > masked) | ((other == masked) & (e < eidx))).astype(jnp.int32)
    sel = jnp.where(erank < TOP_K, scores, 0.0)
    gates = sel / jnp.sum(sel, axis=0, keepdims=True) * ROUTED_SCALE
    return jnp.concatenate([gates, jnp.zeros_like(gates)], axis=0)


def _moe_body(x_ref, wrt_ref, eb_ref, wsg_ref, wsu_ref, wsd_ref, wg_ref, wu_ref, wd_ref, g3_ref, b3_ref,
              o_ref, xb_scr, gate_scr, acc_scr):
    e = pl.program_id(1)

    @pl.when(e == 0)
    def _():
        x = x_ref[...]
        xb = x.astype(BF16)
        xb_scr[...] = xb
        gate_scr[...] = jnp.transpose(_route(x, wrt_ref[...], eb_ref[...]))
        hs = jax.nn.silu(_dot(xb, wsg_ref[...])) * _dot(xb, wsu_ref[...])
        acc_scr[...] = _dot(hs.astype(BF16), wsd_ref[...])

    xb = xb_scr[...]
    lane = lax.broadcasted_iota(jnp.int32, gate_scr.shape, 1)
    acc = acc_scr[...]
    for j in range(MOE_EXPERTS_PER_STEP):
        hg = _dot(xb, wg_ref[j].astype(BF16))
        hu = _dot(xb, wu_ref[j].astype(BF16))
        gcol = jnp.sum(jnp.where(lane == e * MOE_EXPERTS_PER_STEP + j, gate_scr[...], 0.0), axis=1, keepdims=True)
        act = jax.nn.silu(hg) * hu * gcol
        acc = acc + _dot(act.astype(BF16), wd_ref[j].astype(BF16))
    acc_scr[...] = acc

    @pl.when(e == pl.num_programs(1) - 1)
    def _():
        o_ref[...] = _layer_norm(DN_ALPHA * x_ref[...] + acc_scr[...], g3_ref[...], b3_ref[...])


def moe_ln(x, wr_t, e_bias, wsg, wsu, wsd, w_e_gate, w_e_up, w_e_down, g3, b3, tm):
    n = x.shape[0]
    eps = MOE_EXPERTS_PER_STEP
    row = pl.BlockSpec((tm, D_MODEL), lambda i, e: (i, 0))
    full = lambda a: pl.BlockSpec(a.shape, lambda i, e: (0,) * a.ndim)
    return pl.pallas_call(
        _moe_body,
        grid=(n // tm, N_EXPERTS // eps),
        in_specs=[row, full(wr_t), full(e_bias), full(wsg), full(wsu), full(wsd),
                  pl.BlockSpec((eps, D_MODEL, D_EXPERT), lambda i, e: (e, 0, 0)),
                  pl.BlockSpec((eps, D_MODEL, D_EXPERT), lambda i, e: (e, 0, 0)),
                  pl.BlockSpec((eps, D_EXPERT, D_MODEL), lambda i, e: (e, 0, 0)),
                  full(g3), full(b3)],
        out_specs=row,
        out_shape=jax.ShapeDtypeStruct((n, D_MODEL), F32),
        scratch_shapes=[pltpu.VMEM((tm, D_MODEL), BF16), pltpu.VMEM((tm, LANES), F32),
                        pltpu.VMEM((tm, D_MODEL), F32)],
        compiler_params=_cparams(("arbitrary", "arbitrary")),
        name="moe_ln3",
    )(x, wr_t, e_bias, wsg, wsu, wsd, w_e_gate, w_e_up, w_e_down, g3, b3)


def _block_ones(seg):
    idx = jnp.arange(LANES, dtype=jnp.int32) // seg
    return (idx[:, None] == idx[None, :]).astype(BF16)


def _row(v):
    return v.reshape(1, -1).astype(F32)


def kernel(x_prompt, x_sample, cache_k, cache_v, state_wkv, state_shift, cache_mem_k, cache_mem_v, page_table, mem_prompt, ln_in_g, ln_in_b, rel_bias, w_in, mu_shift, w0, w_decay_up, a0, w_aaa_up, w_gate_up, k_k, k_a, r_k, lnx_g, lnx_b, lam_q1, lam_k1, lam_q2, lam_k2, subln_g, w_out, ln1_g, ln1_b, w_mq, w_mk, w_mv, w_mo, ln2_g, ln2_b, w_router, e_bias, w_e_gate, w_e_up, w_e_down, w_s_gate, w_s_up, w_s_down, ln3_g, ln3_b):
    assert w_in.shape[0] == DEPTH == 1
    nb, t, _ = x_prompt.shape
    db, s_len, _ = x_sample.shape
    l = 0
    seg64 = _block_ones(RWKV_HD)
    ones128 = _block_ones(LANES)
    zeros = jnp.zeros((64, 512), F32)
    w_lo = jnp.concatenate([jnp.concatenate([w_decay_up[l], zeros], axis=1),
                            jnp.concatenate([zeros, w_aaa_up[l]], axis=1)], axis=0).astype(BF16)
    prep_consts = (_row(mu_shift[l]), w_lo, _row(jnp.concatenate([w0[l], a0[l]])), w_gate_up[l].astype(BF16),
                   _row(k_k[l]), _row(k_a[l]), _row(r_k[l]), seg64)
    mix_consts = (_row(lnx_g[l]), _row(lnx_b[l]), _row(jnp.tile(subln_g[l] * (1.0 - LAM_INIT), N_DIFF_HEADS)),
                  w_out[l].astype(BF16), _row(ln1_g[l]), _row(ln1_b[l]), seg64, ones128)
    lam = (jnp.exp(jnp.sum(lam_q1[l] * lam_k1[l]).astype(F32)) - jnp.exp(jnp.sum(lam_q2[l] * lam_k2[l]).astype(F32))
           + LAM_INIT).reshape(1, 1)
    w_in_b = w_in[l].astype(BF16)
    wkt_b = jnp.transpose(w_in_b[:, RWKV_COLS + 512:RWKV_COLS + 1024])
    w_mq_b, w_mk_b, w_mv_b, w_mo_b = (w[l].astype(BF16) for w in (w_mq, w_mk, w_mv, w_mo))
    wr_t = jnp.transpose(w_router[l])
    eb = e_bias[l].reshape(N_EXPERTS, 1).astype(F32)
    wsg, wsu, wsd = w_s_gate[l].astype(BF16), w_s_up[l].astype(BF16), w_s_down[l].astype(BF16)
    g_in, b_in = _row(ln_in_g), _row(ln_in_b)
    g2, b2, g3, b3 = _row(ln2_g[l]), _row(ln2_b[l]), _row(ln3_g[l]), _row(ln3_b[l])

    def rwkv(p, prev, nseq, tlen, s0, tm, shift_in_kernel, tc):
        a, w, wr, b, k2, v, br, kr, g, bv = rwkv_prep(p, prev, prep_consts, tm, shift_in_kernel, tlen)
        y, st = wkv_scan(*[_to_scan_j(z, nseq, tlen) for z in (a, w, wr, b, k2, v)],
                         _to_scan_s(br, nseq, tlen), _to_scan_s(kr, nseq, tlen), s0, tc)
        return _from_scan_j(y, nseq, tlen), bv, g, _state_from_scan(st, nseq)

    def tail(h1, om, tm, tm_moe):
        h2 = matmul_res_ln(om, h1, w_mo_b, g2, b2, tm, "mem_out_ln2")
        return moe_ln(h2, wr_t, eb, wsg, wsu, wsd, w_e_gate[l], w_e_up[l], w_e_down[l], g3, b3, tm_moe)

    n_p = nb * t
    tm = 512
    hp, pp, qp, kp, vp = ln_proj(x_prompt.reshape(n_p, D_MODEL), g_in, b_in, w_in_b, wkt_b, tm, seq_len=t)
    s0_p = jnp.zeros((nb * N_RWKV_HEADS // SCAN_BH, SCAN_JLO, RWKV_HD, LANES), F32)
    y_p, bv_p, gt_p, wkv_p = rwkv(pp, None, nb, t, s0_p, tm, True, 128)
    od_p = diff_attn_prompt(qp, kp, vp, rel_bias, lam, nb, t, 512)
    h1_p = mix_ln(y_p, bv_p, gt_p, od_p, hp, mix_consts, tm)
    mem_rows = mem_prompt.reshape(nb * N_MEM, D_MODEL)
    mk_p = matmul(mem_rows, w_mk_b, tm, "mem_k")
    mv_p = matmul(mem_rows, w_mv_b, tm, "mem_v")
    qm_p = matmul(h1_p, w_mq_b, tm, "mem_q").reshape(nb, t, D_MODEL)
    om_p = mem_attn(qm_p, mk_p.reshape(nb, N_MEM, D_MODEL), mv_p.reshape(nb, N_MEM, D_MODEL), tm)
    h3_p = tail(h1_p, om_p.reshape(n_p, D_MODEL), tm, 1024)

    n_s = db * s_len
    hs, ps, qs, ks, vs = ln_proj(x_sample.reshape(n_s, D_MODEL), g_in, b_in, w_in_b, wkt_b, n_s)
    ps3 = ps.reshape(db, s_len, RWKV_COLS)
    prev_s = jnp.concatenate([state_shift[l][:, None, :], ps3[:, :-1]], axis=1).reshape(n_s, RWKV_COLS)
    y_s, bv_s, gt_s, wkv_s = rwkv(ps, prev_s, db, s_len, _state_to_scan(state_wkv[l], db), n_s, False, s_len)
    n_pool = cache_k.shape[1]
    od_s = diff_attn_sample(qs.reshape(db, s_len, 512), ks.reshape(db, s_len, 512), vs.reshape(db, s_len, 512),
                            jnp.transpose(cache_k[l], (0, 2, 3, 4, 1)).reshape(n_pool, DIFF_WIDTH, PAGE_SIZE),
                            cache_v[l].reshape(n_pool, PAGE_SIZE * N_DIFF_HEADS, DIFF_VD),
                            page_table, rel_bias, lam, 32)
    h1_s = mix_ln(y_s, bv_s, gt_s, od_s.reshape(n_s, DIFF_WIDTH), hs, mix_consts, n_s)
    qm_s = matmul(h1_s, w_mq_b, n_s, "mem_q").reshape(db, s_len, D_MODEL)
    qm_s = jnp.pad(qm_s, ((0, 0), (0, SUBLANES - s_len), (0, 0)))
    om_s = mem_attn_cached(qm_s, _mem_cache_rows(cache_mem_k[l]), _mem_cache_rows(cache_mem_v[l]))
    h3_s = tail(h1_s, om_s[:, :s_len].reshape(n_s, D_MODEL), n_s, n_s)

    return (h3_p.reshape(nb, t, D_MODEL), h3_s.reshape(db, s_len, D_MODEL),
            jnp.transpose(kp.reshape(1, nb, N_DIFF_HEADS, 2, DIFF_HD, t), (0, 1, 5, 2, 3, 4)),
            vp.reshape(1, nb, t, N_DIFF_HEADS, DIFF_VD),
            wkv_p[None], pp.reshape(nb, t, RWKV_COLS)[:, -1][None],
            mk_p.reshape(1, nb, N_MEM, N_MEM_HEADS, MEM_HD), mv_p.reshape(1, nb, N_MEM, N_MEM_HEADS, MEM_HD),
            ks.reshape(1, db, s_len, N_DIFF_HEADS, 2, DIFF_HD), vs.reshape(1, db, s_len, N_DIFF_HEADS, DIFF_VD),
            wkv_s[None], ps3[:, -1][None])
```

```python
import functools
import math

import jax
import jax.numpy as jnp
from jax import lax
from jax.experimental import pallas as pl
from jax.experimental.pallas import tpu as pltpu

F32 = jnp.float32
BF16 = jnp.bfloat16

D_MODEL = 1024
RWKV_WIDTH = 512
RWKV_HD = 64
N_RWKV_HEADS = 8
RWKV_COLS = 1792
GN_EPS = 64e-5
DIFF_WIDTH = 512
DIFF_HD = 64
DIFF_VD = 128
N_DIFF_HEADS = 4
RMS_EPS = 1e-5
N_BUCKETS = 32
MAX_DISTANCE = 128
PAGE_SIZE = 128
N_MEM = 256
N_MEM_HEADS = 4
MEM_HD = 256
N_EXPERTS = 64
N_GROUPS = 8
GROUP_SIZE = 8
TOPK_GROUPS = 4
TOP_K = 8
D_EXPERT = 256
ROUTED_SCALE = 2.5
MOE_EXPERTS_PER_STEP = 4
DEPTH = 1
DN_ALPHA = (2.0 * DEPTH) ** 0.25
LN_EPS = 1e-5
NEG_INF = -1e30
LAM_INIT = 0.8 - 0.6 * math.exp(-0.3 * 0)

LANES = 128
SUBLANES = 8
VMEM_LIMIT = 60 * 1024 * 1024

SCAN_JLO = 16
SCAN_JHI = RWKV_HD // SCAN_JLO
SCAN_BH = LANES // SCAN_JHI


def _cparams(sem):
    return pltpu.CompilerParams(dimension_semantics=sem, vmem_limit_bytes=VMEM_LIMIT)


def _layer_norm(x, g, b):
    mu = jnp.mean(x, -1, keepdims=True)
    d = x - mu
    var = jnp.mean(d * d, -1, keepdims=True)
    return d * lax.rsqrt(var + LN_EPS) * g + b


def _dot(a, b):
    return jnp.dot(a, b, preferred_element_type=F32)


def _dot_nt(a, b):
    return lax.dot_general(a, b, (((1,), (1,)), ((), ())), preferred_element_type=F32)


def _seg_sum(x, ones_blk):
    outs = []
    for c in range(x.shape[1] // LANES):
        xc = x[:, c * LANES:(c + 1) * LANES]
        hi = xc.astype(BF16)
        lo = (xc - hi.astype(F32)).astype(BF16)
        outs.append(_dot(hi, ones_blk) + _dot(lo, ones_blk))
    return outs[0] if len(outs) == 1 else jnp.concatenate(outs, axis=1)


def _ln_proj_body(k_transposed, x_ref, g_ref, b_ref, w_ref, wkt_ref, h_ref, p_ref, q_ref, k_ref, v_ref):
    h = _layer_norm(x_ref[...], g_ref[...], b_ref[...])
    h_ref[...] = h
    hb = h.astype(BF16)
    c0 = RWKV_COLS
    p_ref[...] = _dot(hb, w_ref[:, 0:c0])
    q_ref[...] = _dot(hb, w_ref[:, c0:c0 + 512])
    if k_transposed:
        k_ref[0] = _dot_nt(wkt_ref[...], hb)
    else:
        k_ref[...] = _dot(hb, w_ref[:, c0 + 512:c0 + 1024])
    v_ref[...] = _dot(hb, w_ref[:, c0 + 1024:c0 + 1536])


def ln_proj(x, g, b, w_bf16, wkt_bf16, tm, seq_len=None):
    n = x.shape[0]
    row = lambda width: pl.BlockSpec((tm, width), lambda i: (i, 0))
    full = lambda a: pl.BlockSpec(a.shape, lambda i: (0,) * a.ndim)
    if seq_len is None:
        k_spec, k_shape = row(512), (n, 512)
    else:
        bps = seq_len // tm
        k_spec = pl.BlockSpec((1, 512, tm), lambda i: (i // bps, 0, i % bps))
        k_shape = (n // seq_len, 512, seq_len)
    shapes = [(n, D_MODEL), (n, RWKV_COLS), (n, 512), k_shape, (n, 512)]
    return pl.pallas_call(
        functools.partial(_ln_proj_body, seq_len is not None),
        grid=(n // tm,),
        in_specs=[row(D_MODEL), full(g), full(b), full(w_bf16), full(wkt_bf16)],
        out_specs=[row(D_MODEL), row(RWKV_COLS), row(512), k_spec, row(512)],
        out_shape=[jax.ShapeDtypeStruct(sh, F32) for sh in shapes],
        compiler_params=_cparams(("arbitrary",)),
        name="ln_proj",
    )(x, g, b, w_bf16, wkt_bf16)


def _mm_body(x_ref, w_ref, o_ref):
    o_ref[...] = _dot(x_ref[...].astype(BF16), w_ref[...])


def matmul(x, w_bf16, tm, name):
    n, kdim = x.shape
    nout = w_bf16.shape[1]
    return pl.pallas_call(
        _mm_body,
        grid=(n // tm,),
        in_specs=[pl.BlockSpec((tm, kdim), lambda i: (i, 0)), pl.BlockSpec((kdim, nout), lambda i: (0, 0))],
        out_specs=pl.BlockSpec((tm, nout), lambda i: (i, 0)),
        out_shape=jax.ShapeDtypeStruct((n, nout), F32),
        compiler_params=_cparams(("arbitrary",)),
        name=name,
    )(x, w_bf16)


def _mm_res_ln_body(x_ref, res_ref, w_ref, g_ref, b_ref, o_ref):
    y = _dot(x_ref[...].astype(BF16), w_ref[...])
    o_ref[...] = _layer_norm(DN_ALPHA * res_ref[...] + y, g_ref[...], b_ref[...])


def matmul_res_ln(x, res, w_bf16, g, b, tm, name):
    n, kdim = x.shape
    nout = w_bf16.shape[1]
    row = lambda width: pl.BlockSpec((tm, width), lambda i: (i, 0))
    full = lambda a: pl.BlockSpec(a.shape, lambda i: (0,) * a.ndim)
    return pl.pallas_call(
        _mm_res_ln_body,
        grid=(n // tm,),
        in_specs=[row(kdim), row(nout), full(w_bf16), full(g), full(b)],
        out_specs=row(nout),
        out_shape=jax.ShapeDtypeStruct((n, nout), F32),
        compiler_params=_cparams(("arbitrary",)),
        name=name,
    )(x, res, w_bf16, g, b)


def _rwkv_prep_body(shift_in_kernel, seq_blocks, p_ref, prev_ref, mu_ref, wlo_ref, w0a0_ref, wg_ref,
                    kk_ref, ka_ref, rk_ref, seg_ref,
                    a_o, w_o, wr_o, b_o, k_o, v_o, br_o, kr_o, g_o, bv_o):
    p = p_ref[...]
    if shift_in_kernel:
        first = jnp.where(pl.program_id(0) % seq_blocks == 0, 0.0, prev_ref[SUBLANES - 1:SUBLANES, :])
        row = lax.broadcasted_iota(jnp.int32, p.shape, 0)
        p_prev = jnp.where(row == 0, first, pltpu.roll(p, 1, 0))
    else:
        p_prev = prev_ref[...]
    ps = p + (p_prev - p) * mu_ref[...]
    r = ps[:, 0:512]
    k = ps[:, 512:1024]
    v = ps[:, 1024:1536]
    lo = ps[:, 1536:1664]
    g_lo = ps[:, 1664:1792]
    lane = lax.broadcasted_iota(jnp.int32, lo.shape, 1)
    lo = jnp.where(lane < 64, jnp.tanh(lo), lo)
    wa = _dot(lo.astype(BF16), wlo_ref[...]) + w0a0_ref[...]
    x = -wa[:, 0:512]
    softplus = jnp.maximum(x, 0.0) + jnp.log(1.0 + jnp.exp(-jnp.abs(x)))
    decay = jnp.exp(-jnp.exp(-softplus - 0.5))
    a = jax.nn.sigmoid(wa[:, 512:1024])
    g = _dot(jax.nn.sigmoid(g_lo).astype(BF16), wg_ref[...])
    seg = seg_ref[...]
    kk = k * kk_ref[...]
    kk = kk / jnp.maximum(jnp.sqrt(_seg_sum(kk * kk, seg)), 1e-12)
    k2 = k * (1.0 + (a - 1.0) * ka_ref[...])
    b = kk * a
    a_o[...] = -kk
    w_o[...] = decay
    wr_o[...] = decay * r
    b_o[...] = b
    k_o[...] = k2
    v_o[...] = v
    br_o[...] = _seg_sum(b * r, seg)
    kr_o[...] = _seg_sum(k2 * r, seg)
    g_o[...] = g
    bv_o[...] = _seg_sum(r * k2 * rk_ref[...], seg) * v


def rwkv_prep(p, prev, consts, tm, shift_in_kernel, seq_len):
    n = p.shape[0]
    row = lambda width: pl.BlockSpec((tm, width), lambda i: (i, 0))
    full = lambda a: pl.BlockSpec(a.shape, lambda i: (0,) * a.ndim)
    if shift_in_kernel:
        per8 = tm // SUBLANES
        prev_spec = pl.BlockSpec((SUBLANES, RWKV_COLS), lambda i: (jnp.maximum(i * per8 - 1, 0), 0))
        prev = p
    else:
        prev_spec = row(RWKV_COLS)
    body = functools.partial(_rwkv_prep_body, shift_in_kernel, max(seq_len // tm, 1))
    return pl.pallas_call(
        body,
        grid=(n // tm,),
        in_specs=[row(RWKV_COLS), prev_spec] + [full(c) for c in consts],
        out_specs=[row(512)] * 10,
        out_shape=[jax.ShapeDtypeStruct((n, 512), F32)] * 10,
        compiler_params=_cparams(("arbitrary",)),
        name="rwkv_prep",
    )(p, prev, *consts)


def _lane_fold(x):
    return x + pltpu.roll(x, 32, 1) + pltpu.roll(x, 64, 1) + pltpu.roll(x, 96, 1)


def _wkv_scan_body(tc, a_ref, w_ref, wr_ref, b_ref, k_ref, v_ref, br_ref, kr_ref, s0_ref,
                   y_ref, st_ref, s_scr):
    c = pl.program_id(1)
    ngrp = RWKV_HD // SUBLANES

    @pl.when(c == 0)
    def _():
        s_scr[...] = s0_ref[0]

    def bcast(ref, t, j):
        return jnp.broadcast_to(ref[0, j, pl.ds(t, 1), :], (SUBLANES, LANES))

    lane_grp = lax.broadcasted_iota(jnp.int32, (SUBLANES, LANES), 1) // SCAN_BH

    def step(t, carry):
        acc_a = [jnp.zeros((SUBLANES, LANES), F32) for _ in range(ngrp)]
        acc_y = [jnp.zeros((SUBLANES, LANES), F32) for _ in range(ngrp)]
        for j in range(SCAN_JLO):
            ab = bcast(a_ref, t, j)
            wrb = bcast(wr_ref, t, j)
            for ig in range(ngrp):
                s = s_scr[j, ig * SUBLANES:(ig + 1) * SUBLANES, :]
                acc_a[ig] = acc_a[ig] + s * ab
                acc_y[ig] = acc_y[ig] + s * wrb
        sa = [_lane_fold(x) for x in acc_a]
        vp = [v_ref[0, t, m * SUBLANES:(m + 1) * SUBLANES, :] for m in range(2)]
        vrot = [[x] + [pltpu.roll(x, SCAN_BH * k, 1) for k in range(1, SCAN_JHI)] for x in vp]
        vv = []
        for ig in range(ngrp):
            q, rot = ig // 2, vrot[ig % 2]
            x = rot[(SCAN_JHI - 1 - q) % SCAN_JHI]
            for g in range(SCAN_JHI - 2, -1, -1):
                x = jnp.where(lane_grp == g, rot[(g - q) % SCAN_JHI], x)
            vv.append(x)
        for j in range(SCAN_JLO):
            wb = bcast(w_ref, t, j)
            bb = bcast(b_ref, t, j)
            kb = bcast(k_ref, t, j)
            for ig in range(ngrp):
                sl = slice(ig * SUBLANES, (ig + 1) * SUBLANES)
                s_scr[j, sl, :] = s_scr[j, sl, :] * wb + sa[ig] * bb + vv[ig] * kb
        br = jnp.broadcast_to(br_ref[0, t], (SUBLANES, LANES))
        kr = jnp.broadcast_to(kr_ref[0, t], (SUBLANES, LANES))
        y = [_lane_fold(acc_y[ig]) + sa[ig] * br + vv[ig] * kr for ig in range(ngrp)]
        for m in range(2):
            yp = y[6 + m]
            for q in (2, 1, 0):
                yp = jnp.where(lane_grp == q, y[2 * q + m], yp)
            y_ref[0, t, m * SUBLANES:(m + 1) * SUBLANES, :] = yp
        return carry

    lax.fori_loop(0, tc, step, 0)

    @pl.when(c == pl.num_programs(1) - 1)
    def _():
        st_ref[0] = s_scr[...]


def wkv_scan(a, w, wr, b, k, v, br, kr, s0, tc):
    g, t = v.shape[0], v.shape[1]
    jspec = pl.BlockSpec((1, SCAN_JLO, tc, LANES), lambda gi, c: (gi, 0, c, 0))
    tspec = pl.BlockSpec((1, tc, SCAN_JLO, LANES), lambda gi, c: (gi, c, 0, 0))
    sspec = pl.BlockSpec((1, tc, 1, LANES), lambda gi, c: (gi, c, 0, 0))
    stspec = pl.BlockSpec((1, SCAN_JLO, RWKV_HD, LANES), lambda gi, c: (gi, 0, 0, 0))
    return pl.pallas_call(
        functools.partial(_wkv_scan_body, tc),
        grid=(g, t // tc),
        in_specs=[jspec] * 5 + [tspec, sspec, sspec, stspec],
        out_specs=[tspec, stspec],
        out_shape=[jax.ShapeDtypeStruct((g, t, SCAN_JLO, LANES), F32),
                   jax.ShapeDtypeStruct((g, SCAN_JLO, RWKV_HD, LANES), F32)],
        scratch_shapes=[pltpu.VMEM((SCAN_JLO, RWKV_HD, LANES), F32)],
        compiler_params=_cparams(("arbitrary", "arbitrary")),
        name="wkv_scan",
    )(a, w, wr, b, k, v, br, kr, s0)


def _to_scan_t(x, nb, t):
    g = nb * N_RWKV_HEADS // SCAN_BH
    bp = nb // g
    x = x.reshape(g, bp, t, N_RWKV_HEADS, SCAN_JHI, SCAN_JLO)
    return jnp.transpose(x, (0, 2, 5, 4, 1, 3)).reshape(g, t, SCAN_JLO, LANES)


def _to_scan_j(x, nb, t):
    g = nb * N_RWKV_HEADS // SCAN_BH
    bp = nb // g
    x = x.reshape(g, bp, t, N_RWKV_HEADS, SCAN_JHI, SCAN_JLO)
    return jnp.transpose(x, (0, 5, 2, 4, 1, 3)).reshape(g, SCAN_JLO, t, LANES)


RELAYOUT_PITCH = RWKV_HD + SUBLANES


def _scan_relayout_body(nq, nb, *refs):
    in_refs, out_refs, scr = refs[0:nq], refs[nq:2 * nq], refs[2 * nq]
    for q in range(nq):
        for b in range(nb):
            xt = jnp.transpose(in_refs[q][b])
            for h in range(N_RWKV_HEADS):
                scr[b, h * RELAYOUT_PITCH:h * RELAYOUT_PITCH + RWKV_HD, :] = xt[h * RWKV_HD:(h + 1) * RWKV_HD, :]
        for j in range(SCAN_JLO):
            rows = [scr[b, pl.ds(jq * SCAN_JLO + j, N_RWKV_HEADS, stride=RELAYOUT_PITCH), :]
                    for jq in range(SCAN_JHI) for b in range(nb)]
            out_refs[q][0, j] = jnp.transpose(jnp.concatenate(rows, axis=0))


def scan_relayout(xs, nb, t, tt):
    assert nb * N_RWKV_HEADS == SCAN_BH
    nq = len(xs)
    in_spec = pl.BlockSpec((nb, tt, RWKV_WIDTH), lambda i: (0, i, 0))
    out_spec = pl.BlockSpec((1, SCAN_JLO, tt, LANES), lambda i: (0, 0, i, 0))
    return pl.pallas_call(
        functools.partial(_scan_relayout_body, nq, nb),
        grid=(t // tt,),
        in_specs=[in_spec] * nq,
        out_specs=[out_spec] * nq,
        out_shape=[jax.ShapeDtypeStruct((1, SCAN_JLO, t, LANES), F32)] * nq,
        scratch_shapes=[pltpu.VMEM((nb, N_RWKV_HEADS * RELAYOUT_PITCH, tt), F32)],
        compiler_params=_cparams(("arbitrary",)),
        name="scan_relayout",
    )(*[x.reshape(nb, t, RWKV_WIDTH) for x in xs])


def _from_scan_j(x, nb, t):
    g = x.shape[0]
    bp = nb // g
    x = x.reshape(g, t, SCAN_JLO, SCAN_JHI, bp, N_RWKV_HEADS)
    return jnp.transpose(x, (0, 4, 1, 5, 3, 2)).reshape(nb * t, RWKV_WIDTH)


def _to_scan_s(x, nb, t):
    g = nb * N_RWKV_HEADS // SCAN_BH
    bp = nb // g
    x = x.reshape(g, bp, t, N_RWKV_HEADS, RWKV_HD)[..., 0]
    x = jnp.transpose(x, (0, 2, 1, 3)).reshape(g, t, 1, 1, SCAN_BH)
    return jnp.broadcast_to(x, (g, t, 1, SCAN_JHI, SCAN_BH)).reshape(g, t, 1, LANES)


def _state_to_scan(s, nb):
    g = nb * N_RWKV_HEADS // SCAN_BH
    bp = nb // g
    s = s.reshape(g, bp, N_RWKV_HEADS, RWKV_HD, SCAN_JHI, SCAN_JLO)
    return jnp.transpose(s, (0, 5, 3, 4, 1, 2)).reshape(g, SCAN_JLO, RWKV_HD, LANES)


def _state_from_scan(s, nb):
    g = s.shape[0]
    bp = nb // g
    s = s.reshape(g, SCAN_JLO, RWKV_HD, SCAN_JHI, bp, N_RWKV_HEADS)
    return jnp.transpose(s, (0, 4, 5, 2, 3, 1)).reshape(nb, N_RWKV_HEADS, RWKV_HD, RWKV_HD)


def _mix_body(y_ref, bv_ref, g_ref, od_ref, h_ref, lnxg_ref, lnxb_ref, sub_ref, wout_ref, g1_ref, b1_ref,
              seg_ref, ones_ref, o_ref):
    seg = seg_ref[...]
    y = y_ref[...]
    mu = _seg_sum(y, seg) * (1.0 / RWKV_HD)
    d = y - mu
    var = _seg_sum(d * d, seg) * (1.0 / RWKV_HD)
    yn = d * lax.rsqrt(var + GN_EPS) * lnxg_ref[...] + lnxb_ref[...]
    o_rw = (yn + bv_ref[...]) * g_ref[...]
    od = od_ref[...]
    ms = _seg_sum(od * od, ones_ref[...]) * (1.0 / DIFF_VD)
    o_df = od * lax.rsqrt(ms + RMS_EPS) * sub_ref[...]
    mix = _dot(o_rw.astype(BF16), wout_ref[0:512, :]) + _dot(o_df.astype(BF16), wout_ref[512:1024, :])
    o_ref[...] = _layer_norm(DN_ALPHA * h_ref[...] + mix, g1_ref[...], b1_ref[...])


def mix_ln(y, bv, g, od, h, consts, tm):
    n = y.shape[0]
    row = lambda width: pl.BlockSpec((tm, width), lambda i: (i, 0))
    full = lambda a: pl.BlockSpec(a.shape, lambda i: (0,) * a.ndim)
    return pl.pallas_call(
        _mix_body,
        grid=(n // tm,),
        in_specs=[row(512)] * 4 + [row(D_MODEL)] + [full(c) for c in consts],
        out_specs=row(D_MODEL),
        out_shape=jax.ShapeDtypeStruct((n, D_MODEL), F32),
        compiler_params=_cparams(("arbitrary",)),
        name="mix_ln1",
    )(y, bv, g, od, h, *consts)


def _t5_bucket(dist):
    max_exact = N_BUCKETS // 2
    d = jnp.maximum(dist, 1).astype(F32)
    large = max_exact + (jnp.log(d / max_exact) / math.log(MAX_DISTANCE / max_exact)
                         * (N_BUCKETS - max_exact)).astype(jnp.int32)
    return jnp.where(dist < max_exact, dist, jnp.minimum(large, N_BUCKETS - 1))


def _bias_table(rel_bias, dist):
    bucket = _t5_bucket(jnp.maximum(dist, 0))
    tail = (1,) * dist.ndim
    bias = jnp.zeros(rel_bias.shape[1:] + dist.shape, F32)
    for bk in range(N_BUCKETS):
        bias = jnp.where(bucket == bk, rel_bias[bk].astype(F32).reshape(rel_bias.shape[1:] + tail), bias)
    return jnp.where(dist >= 0, bias, NEG_INF)


def _bias_tiles_body(tq, tk, thr_ref, val_ref, o_ref):
    h = pl.program_id(0)
    rr = lax.broadcasted_iota(jnp.int32, (tq, tk), 0)
    cc = lax.broadcasted_iota(jnp.int32, (tq, tk), 1)
    for kind, base in enumerate((tq, 0)):
        d = base + rr - cc
        for m in range(2):
            bias = jnp.zeros((tq, tk), F32) + val_ref[h * 2 + m]
            for bk in range(1, N_BUCKETS):
                bias = jnp.where(d >= thr_ref[bk], val_ref[(bk * N_DIFF_HEADS + h) * 2 + m], bias)
            o_ref[0, kind, m * tq:(m + 1) * tq, :] = jnp.where(d >= 0, bias, NEG_INF)


def bias_tiles(rel_bias, tq, tk):
    dd = jnp.arange(MAX_DISTANCE + 1, dtype=jnp.int32)
    bucket = _t5_bucket(dd)
    thr = jnp.sum(bucket[None, :] < jnp.arange(N_BUCKETS, dtype=jnp.int32)[:, None], axis=1).astype(jnp.int32)
    smem = pl.BlockSpec(memory_space=pltpu.SMEM)
    return pl.pallas_call(
        functools.partial(_bias_tiles_body, tq, tk),
        grid=(N_DIFF_HEADS,),
        in_specs=[smem, smem],
        out_specs=pl.BlockSpec((1, 2, 2 * tq, tk), lambda h: (h, 0, 0, 0)),
        out_shape=jax.ShapeDtypeStruct((N_DIFF_HEADS, 2, 2 * tq, tk), F32),
        compiler_params=_cparams(("arbitrary",)),
        name="bias_tiles",
    )(thr, rel_bias.astype(F32).reshape(-1))


def _dap_body(tq, tk, qi_ref, ki_ref, lam_ref, q_ref, k_ref, v_ref, bias_ref, cfar_ref, o_ref,
              qs_scr, m_scr, l_scr, acc_scr):
    step = pl.program_id(2)
    qi = qi_ref[step]
    ki = ki_ref[step]

    @pl.when(ki == 0)
    def _():
        q = q_ref[...] * (DIFF_HD ** -0.5)
        lane = lax.broadcasted_iota(jnp.int32, q.shape, 1)
        qs_scr[0:tq, :] = jnp.where(lane < DIFF_HD, q, 0.0).astype(BF16)
        qs_scr[tq:2 * tq, :] = jnp.where(lane >= DIFF_HD, q, 0.0).astype(BF16)
        m_scr[...] = jnp.full(m_scr.shape, NEG_INF, F32)
        l_scr[...] = jnp.zeros(l_scr.shape, F32)
        acc_scr[...] = jnp.zeros(acc_scr.shape, F32)

    def update(s, shift):
        m_prev = m_scr[...]
        m_new = jnp.maximum(m_prev, jnp.max(s, axis=1, keepdims=True) + shift)
        alpha = jnp.exp(m_prev - m_new)
        p = jnp.exp(s - jnp.tile(m_new - shift, (1, tk // LANES)))
        l_scr[...] = alpha * l_scr[...] + jnp.sum(p, axis=1, keepdims=True)
        acc_scr[...] = alpha * acc_scr[...] + _dot(p.astype(BF16), v_ref[...].astype(BF16))
        m_scr[...] = m_new

    @pl.when(ki < qi - 1)
    def _():
        update(_dot(qs_scr[...], k_ref[0].astype(BF16)), cfar_ref[0])

    @pl.when(ki >= qi - 1)
    def _():
        update(_dot(qs_scr[...], k_ref[0].astype(BF16)) + bias_ref[0, 0], jnp.zeros((2 * tq, LANES), F32))

    @pl.when(ki == qi)
    def _():
        o = acc_scr[...] / l_scr[...]
        o_ref[...] = o[0:tq] - lam_ref[0, 0] * o[tq:2 * tq]


def diff_attn_prompt(q, kt, v, rel_bias, lam, nb, t, tq):
    tk = tq
    assert tq >= MAX_DISTANCE and t % tq == 0
    nq = t // tq
    qi_list, ki_list = [], []
    for a in range(nq):
        for c in range(a + 1):
            qi_list.append(a)
            ki_list.append(c)
    qi_arr = jnp.asarray(qi_list, jnp.int32)
    ki_arr = jnp.asarray(ki_list, jnp.int32)
    tiles = bias_tiles(rel_bias, tq, tk)
    cfar = jnp.broadcast_to(rel_bias[N_BUCKETS - 1].astype(F32)[:, :, None, None], (N_DIFF_HEADS, 2, tq, LANES))
    cfar = cfar.reshape(N_DIFF_HEADS, 2 * tq, LANES)

    grid_spec = pltpu.PrefetchScalarGridSpec(
        num_scalar_prefetch=2,
        grid=(nb, N_DIFF_HEADS, len(qi_list)),
        in_specs=[
            pl.BlockSpec(memory_space=pltpu.SMEM),
            pl.BlockSpec((tq, LANES), lambda b, h, s, qi, ki: (b * nq + qi[s], h)),
            pl.BlockSpec((1, LANES, tk), lambda b, h, s, qi, ki: (b, h, ki[s])),
            pl.BlockSpec((tk, LANES), lambda b, h, s, qi, ki: (b * nq + ki[s], h)),
            pl.BlockSpec((1, 1, 2 * tq, tk), lambda b, h, s, qi, ki: (h, jnp.where(ki[s] == qi[s], 1, 0), 0, 0)),
            pl.BlockSpec((1, 2 * tq, LANES), lambda b, h, s, qi, ki: (h, 0, 0)),
        ],
        out_specs=pl.BlockSpec((tq, LANES), lambda b, h, s, qi, ki: (b * nq + qi[s], h)),
        scratch_shapes=[pltpu.VMEM((2 * tq, LANES), BF16), pltpu.VMEM((2 * tq, LANES), F32),
                        pltpu.VMEM((2 * tq, LANES), F32), pltpu.VMEM((2 * tq, LANES), F32)],
    )
    return pl.pallas_call(
        functools.partial(_dap_body, tq, tk),
        grid_spec=grid_spec,
        out_shape=jax.ShapeDtypeStruct((nb * t, DIFF_WIDTH), F32),
        compiler_params=_cparams(("arbitrary", "arbitrary", "arbitrary")),
        name="diff_attn_prompt",
    )(qi_arr, ki_arr, lam, q, kt, v, tiles, cfar)


def _das_body(npg, s_len, pt_ref, lam_ref, q_ref, qmask_ref, bias_ref, bnew_ref, kn_ref, vn_ref, *rest):
    k_refs = rest[0:npg]
    v_refs = rest[npg:2 * npg]
    o_ref = rest[2 * npg]
    qs_scr, m_scr, l_scr, acc_scr = rest[2 * npg + 1:]
    g = pl.program_id(1)
    rows = 2 * N_DIFF_HEADS * s_len

    @pl.when(g == 0)
    def _():
        q = q_ref[0] * (DIFF_HD ** -0.5)
        qrep = jnp.concatenate([q] * (2 * N_DIFF_HEADS), axis=0)
        qs_scr[...] = (qrep * qmask_ref[...]).astype(BF16)
        m_scr[...] = jnp.full(m_scr.shape, NEG_INF, F32)
        l_scr[...] = jnp.zeros(l_scr.shape, F32)
        acc_scr[...] = jnp.zeros(acc_scr.shape, F32)

    hrows = 2 * s_len
    qs = qs_scr[...]
    s = jnp.concatenate([_dot(qs, kr[0].astype(BF16)) for kr in k_refs], axis=1) + bias_ref[0]
    m_prev = m_scr[...]
    m_new = jnp.maximum(m_prev, jnp.max(s, axis=1, keepdims=True))
    alpha = jnp.exp(m_prev - m_new)
    p = jnp.exp(s - jnp.tile(m_new, (1, npg)))
    l_scr[...] = alpha * l_scr[...] + jnp.sum(p, axis=1, keepdims=True)
    pvs = []
    for h in range(N_DIFF_HEADS):
        pv = None
        for j in range(npg):
            ph = p[h * hrows:(h + 1) * hrows, j * PAGE_SIZE:(j + 1) * PAGE_SIZE].astype(BF16)
            vh = v_refs[j][0, pl.ds(h, PAGE_SIZE, stride=N_DIFF_HEADS), :].astype(BF16)
            d = _dot(ph, vh)
            pv = d if pv is None else pv + d
        pvs.append(pv)
    acc_scr[...] = alpha * acc_scr[...] + jnp.concatenate(pvs, axis=0)
    m_scr[...] = m_new

    @pl.when(g == pl.num_programs(1) - 1)
    def _():
        qf = qs_scr[...].astype(F32)
        kn = kn_ref[0].astype(BF16).astype(F32)
        vn = vn_ref[0].astype(BF16).astype(F32)
        lane = lax.broadcasted_iota(jnp.int32, (rows, LANES), 1)
        sn = bnew_ref[...]
        for j in range(s_len):
            dj = jnp.sum(qf * kn[j:j + 1, :], axis=1, keepdims=True)
            sn = sn + jnp.where(lane == j, dj, 0.0)
        m_prev = m_scr[...]
        m_new = jnp.maximum(m_prev, jnp.max(sn, axis=1, keepdims=True))
        alpha = jnp.exp(m_prev - m_new)
        pn = jnp.exp(sn - m_new).astype(BF16).astype(F32)
        l_fin = alpha * l_scr[...] + jnp.sum(pn, axis=1, keepdims=True)
        acc = alpha * acc_scr[...]
        for j in range(s_len):
            vj = jnp.concatenate([jnp.broadcast_to(vn[j:j + 1, h * DIFF_VD:(h + 1) * DIFF_VD], (hrows, DIFF_VD))
                                  for h in range(N_DIFF_HEADS)], axis=0)
            acc = acc + pn[:, j:j + 1] * vj
        o = acc / l_fin
        lam = lam_ref[0, 0]
        outs = []
        for h in range(N_DIFF_HEADS):
            blk = o[h * hrows:(h + 1) * hrows, :]
            outs.append(blk[0:s_len] - lam * blk[s_len:hrows])
        o_ref[0] = jnp.concatenate(outs, axis=1)


def diff_attn_sample(q, k, v, cache_k, cache_v, page_table, rel_bias, lam, npg):
    db, s_len, _ = q.shape
    n_pages = page_table.shape[1]
    past = n_pages * PAGE_SIZE
    assert n_pages % npg == 0
    ngroups = n_pages // npg
    rows = 2 * N_DIFF_HEADS * s_len
    width = npg * PAGE_SIZE
    hm = jnp.arange(rows, dtype=jnp.int32) // s_len
    qmask = (jnp.arange(DIFF_WIDTH, dtype=jnp.int32)[None, :] // DIFF_HD == hm[:, None]).astype(F32)
    qpos = past + jnp.arange(s_len, dtype=jnp.int32)

    def rows_of(table):
        return table.reshape(rows, table.shape[-1])

    far = rows_of(_bias_table(rel_bias, jnp.full((s_len, width), 2 * MAX_DISTANCE, jnp.int32)))
    kpos_last = past - width + jnp.arange(width, dtype=jnp.int32)
    last = rows_of(_bias_table(rel_bias, qpos[:, None] - kpos_last[None, :]))
    bias = jnp.stack([far, last], axis=0)
    knew_pos = past + jnp.arange(LANES, dtype=jnp.int32)
    dist_new = jnp.where(jnp.arange(LANES)[None, :] < s_len, qpos[:, None] - knew_pos[None, :], -1)
    bnew = rows_of(_bias_table(rel_bias, dist_new))
    assert width >= MAX_DISTANCE + s_len
    pt_flat = page_table.reshape(-1).astype(jnp.int32)

    def page_spec(j):
        return pl.BlockSpec((1, DIFF_WIDTH, PAGE_SIZE),
                            lambda b, g, pt, j=j: (pt[b * n_pages + g * npg + j], 0, 0))

    seq_spec = pl.BlockSpec((1, s_len, DIFF_WIDTH), lambda b, g, pt: (b, 0, 0))
    full2 = lambda a: pl.BlockSpec(a.shape, lambda b, g, pt: (0,) * a.ndim)
    grid_spec = pltpu.PrefetchScalarGridSpec(
        num_scalar_prefetch=1,
        grid=(db, ngroups),
        in_specs=[pl.BlockSpec(memory_space=pltpu.SMEM), seq_spec, full2(qmask),
                  pl.BlockSpec((1, rows, width), lambda b, g, pt: (jnp.where(g == ngroups - 1, 1, 0), 0, 0)),
                  full2(bnew), seq_spec, seq_spec]
                 + [page_spec(j) for j in range(npg)] + [page_spec(j) for j in range(npg)],
        out_specs=seq_spec,
        scratch_shapes=[pltpu.VMEM((rows, DIFF_WIDTH), BF16), pltpu.VMEM((rows, LANES), F32),
                        pltpu.VMEM((rows, LANES), F32), pltpu.VMEM((rows, DIFF_VD), F32)],
    )
    return pl.pallas_call(
        functools.partial(_das_body, npg, s_len),
        grid_spec=grid_spec,
        out_shape=jax.ShapeDtypeStruct((db, s_len, DIFF_WIDTH), F32),
        compiler_params=_cparams(("arbitrary", "arbitrary")),
        name="diff_attn_sample",
    )(pt_flat, lam, q, qmask, bias, bnew, k, v, *([cache_k] * npg), *([cache_v] * npg))


def _mem_attn_body(q_ref, mk_ref, mv_ref, o_ref):
    for h in range(N_MEM_HEADS):
        cols = slice(h * MEM_HD, (h + 1) * MEM_HD)
        qh = (q_ref[0, :, cols] * (MEM_HD ** -0.5)).astype(BF16)
        s = _dot_nt(qh, mk_ref[0, :, cols].astype(BF16))
        p = jnp.exp(s - jnp.max(s, axis=1, keepdims=True))
        l = jnp.sum(p, axis=1, keepdims=True)
        o_ref[0, :, cols] = _dot(p.astype(BF16), mv_ref[0, :, cols].astype(BF16)) / l


def mem_attn(q, mk, mv, tm):
    nb, t, _ = q.shape
    qspec = pl.BlockSpec((1, tm, D_MODEL), lambda b, i: (b, i, 0))
    mspec = pl.BlockSpec((1, N_MEM, D_MODEL), lambda b, i: (b, 0, 0))
    return pl.pallas_call(
        _mem_attn_body,
        grid=(nb, t // tm),
        in_specs=[qspec, mspec, mspec],
        out_specs=qspec,
        out_shape=jax.ShapeDtypeStruct((nb, t, D_MODEL), F32),
        compiler_params=_cparams(("arbitrary", "arbitrary")),
        name="mem_attn",
    )(q, mk, mv)


MEM_COL_TILES = MEM_HD // LANES


def _mem_cache_rows(cache):
    nb = cache.shape[0]
    c = cache.reshape(nb, N_MEM, N_MEM_HEADS, MEM_COL_TILES, LANES)
    return jnp.transpose(c, (0, 1, 3, 2, 4)).reshape(nb, N_MEM * MEM_COL_TILES * N_MEM_HEADS, LANES)


def _mem_attn_cached_body(q_ref, mk_ref, mv_ref, o_ref):
    stride = MEM_COL_TILES * N_MEM_HEADS

    def tile(ref, h, ct):
        return ref[0, pl.ds(ct * N_MEM_HEADS + h, N_MEM, stride=stride), :].astype(BF16)

    for h in range(N_MEM_HEADS):
        s = None
        for ct in range(MEM_COL_TILES):
            c0 = h * MEM_HD + ct * LANES
            qh = (q_ref[0, :, c0:c0 + LANES] * (MEM_HD ** -0.5)).astype(BF16)
            d = _dot_nt(qh, tile(mk_ref, h, ct))
            s = d if s is None else s + d
        p = jnp.exp(s - jnp.max(s, axis=1, keepdims=True))
        l = jnp.sum(p, axis=1, keepdims=True)
        pb = p.astype(BF16)
        for ct in range(MEM_COL_TILES):
            c0 = h * MEM_HD + ct * LANES
            o_ref[0, :, c0:c0 + LANES] = _dot(pb, tile(mv_ref, h, ct)) / l


def mem_attn_cached(q, mk_rows, mv_rows):
    nb, t, _ = q.shape
    qspec = pl.BlockSpec((1, t, D_MODEL), lambda b: (b, 0, 0))
    mspec = pl.BlockSpec((1,) + mk_rows.shape[1:], lambda b: (b, 0, 0))
    return pl.pallas_call(
        _mem_attn_cached_body,
        grid=(nb,),
        in_specs=[qspec, mspec, mspec],
        out_specs=qspec,
        out_shape=jax.ShapeDtypeStruct((nb, t, D_MODEL), F32),
        compiler_params=_cparams(("arbitrary",)),
        name="mem_attn_cached",
    )(q, mk_rows, mv_rows)


def _route(x, wr_t, e_bias):
    tm = x.shape[0]
    logits = lax.dot_general(wr_t, x, (((1,), (1,)), ((), ())), precision=lax.Precision.HIGHEST,
                             preferred_element_type=F32)
    scores = jax.nn.sigmoid(logits)
    biased = scores + e_bias
    sub = lax.broadcasted_iota(jnp.int32, (GROUP_SIZE, tm), 0)
    grp_rows = []
    for gi in range(N_GROUPS):
        xg = biased[gi * GROUP_SIZE:(gi + 1) * GROUP_SIZE, :]
        m1 = jnp.max(xg, axis=0, keepdims=True)
        i1 = jnp.min(jnp.where(xg == m1, sub, GROUP_SIZE), axis=0, keepdims=True)
        m2 = jnp.max(jnp.where(sub == i1, -jnp.inf, xg), axis=0, keepdims=True)
        grp_rows.append(m1 + m2)
    grp = jnp.concatenate(grp_rows, axis=0)
    gidx = lax.broadcasted_iota(jnp.int32, (N_GROUPS, tm), 0)
    grank = jnp.zeros((N_GROUPS, tm), jnp.int32)
    for gi in range(N_GROUPS):
        other = jnp.broadcast_to(grp[gi:gi + 1, :], (N_GROUPS, tm))
        grank = grank + ((other > grp) | ((other == grp) & (gi < gidx))).astype(jnp.int32)
    gsel = grank < TOPK_GROUPS
    emask = jnp.concatenate([jnp.broadcast_to(gsel[gi:gi + 1, :], (GROUP_SIZE, tm)) for gi in range(N_GROUPS)],
                            axis=0)
    masked = jnp.where(emask, biased, NEG_INF)
    eidx = lax.broadcasted_iota(jnp.int32, (N_EXPERTS, tm), 0)
    erank = jnp.zeros((N_EXPERTS, tm), jnp.int32)
    for e in range(N_EXPERTS):
        other = jnp.broadcast_to(masked[e:e + 1, :], (N_EXPERTS, tm))
        erank = erank + ((other > masked) | ((other == masked) & (e < eidx))).astype(jnp.int32)
    sel = jnp.where(erank < TOP_K, scores, 0.0)
    gates = sel / jnp.sum(sel, axis=0, keepdims=True) * ROUTED_SCALE
    return jnp.concatenate([gates, jnp.zeros_like(gates)], axis=0)


def _moe_body(x_ref, wrt_ref, eb_ref, wsg_ref, wsu_ref, wsd_ref, wg_ref, wu_ref, wd_ref, g3_ref, b3_ref,
              o_ref, xb_scr, gate_scr, acc_scr):
    e = pl.program_id(1)

    @pl.when(e == 0)
    def _():
        x = x_ref[...]
        xb = x.astype(BF16)
        xb_scr[...] = xb
        gate_scr[...] = jnp.transpose(_route(x, wrt_ref[...], eb_ref[...]))
        hs = jax.nn.silu(_dot(xb, wsg_ref[...])) * _dot(xb, wsu_ref[...])
        acc_scr[...] = _dot(hs.astype(BF16), wsd_ref[...])

    xb = xb_scr[...]
    lane = lax.broadcasted_iota(jnp.int32, gate_scr.shape, 1)
    acc = acc_scr[...]
    for j in range(MOE_EXPERTS_PER_STEP):
        hg = _dot(xb, wg_ref[j].astype(BF16))
        hu = _dot(xb, wu_ref[j].astype(BF16))
        gcol = jnp.sum(jnp.where(lane == e * MOE_EXPERTS_PER_STEP + j, gate_scr[...], 0.0), axis=1, keepdims=True)
        act = jax.nn.silu(hg) * hu * gcol
        acc = acc + _dot(act.astype(BF16), wd_ref[j].astype(BF16))
    acc_scr[...] = acc

    @pl.when(e == pl.num_programs(1) - 1)
    def _():
        o_ref[...] = _layer_norm(DN_ALPHA * x_ref[...] + acc_scr[...], g3_ref[...], b3_ref[...])


def moe_ln(x, wr_t, e_bias, wsg, wsu, wsd, w_e_gate, w_e_up, w_e_down, g3, b3, tm):
    n = x.shape[0]
    eps = MOE_EXPERTS_PER_STEP
    row = pl.BlockSpec((tm, D_MODEL), lambda i, e: (i, 0))
    full = lambda a: pl.BlockSpec(a.shape, lambda i, e: (0,) * a.ndim)
    return pl.pallas_call(
        _moe_body,
        grid=(n // tm, N_EXPERTS // eps),
        in_specs=[row, full(wr_t), full(e_bias), full(wsg), full(wsu), full(wsd),
                  pl.BlockSpec((eps, D_MODEL, D_EXPERT), lambda i, e: (e, 0, 0)),
                  pl.BlockSpec((eps, D_MODEL, D_EXPERT), lambda i, e: (e, 0, 0)),
                  pl.BlockSpec((eps, D_EXPERT, D_MODEL), lambda i, e: (e, 0, 0)),
                  full(g3), full(b3)],
        out_specs=row,
        out_shape=jax.ShapeDtypeStruct((n, D_MODEL), F32),
        scratch_shapes=[pltpu.VMEM((tm, D_MODEL), BF16), pltpu.VMEM((tm, LANES), F32),
                        pltpu.VMEM((tm, D_MODEL), F32)],
        compiler_params=_cparams(("arbitrary", "arbitrary")),
        name="moe_ln3",
    )(x, wr_t, e_bias, wsg, wsu, wsd, w_e_gate, w_e_up, w_e_down, g3, b3)


def _block_ones(seg):
    idx = jnp.arange(LANES, dtype=jnp.int32) // seg
    return (idx[:, None] == idx[None, :]).astype(BF16)


def _row(v):
    return v.reshape(1, -1).astype(F32)


def kernel(x_prompt, x_sample, cache_k, cache_v, state_wkv, state_shift, cache_mem_k, cache_mem_v, page_table, mem_prompt, ln_in_g, ln_in_b, rel_bias, w_in, mu_shift, w0, w_decay_up, a0, w_aaa_up, w_gate_up, k_k, k_a, r_k, lnx_g, lnx_b, lam_q1, lam_k1, lam_q2, lam_k2, subln_g, w_out, ln1_g, ln1_b, w_mq, w_mk, w_mv, w_mo, ln2_g, ln2_b, w_router, e_bias, w_e_gate, w_e_up, w_e_down, w_s_gate, w_s_up, w_s_down, ln3_g, ln3_b):
    assert w_in.shape[0] == DEPTH == 1
    nb, t, _ = x_prompt.shape
    db, s_len, _ = x_sample.shape
    l = 0
    seg64 = _block_ones(RWKV_HD)
    ones128 = _block_ones(LANES)
    zeros = jnp.zeros((64, 512), F32)
    w_lo = jnp.concatenate([jnp.concatenate([w_decay_up[l], zeros], axis=1),
                            jnp.concatenate([zeros, w_aaa_up[l]], axis=1)], axis=0).astype(BF16)
    prep_consts = (_row(mu_shift[l]), w_lo, _row(jnp.concatenate([w0[l], a0[l]])), w_gate_up[l].astype(BF16),
                   _row(k_k[l]), _row(k_a[l]), _row(r_k[l]), seg64)
    mix_consts = (_row(lnx_g[l]), _row(lnx_b[l]), _row(jnp.tile(subln_g[l] * (1.0 - LAM_INIT), N_DIFF_HEADS)),
                  w_out[l].astype(BF16), _row(ln1_g[l]), _row(ln1_b[l]), seg64, ones128)
    lam = (jnp.exp(jnp.sum(lam_q1[l] * lam_k1[l]).astype(F32)) - jnp.exp(jnp.sum(lam_q2[l] * lam_k2[l]).astype(F32))
           + LAM_INIT).reshape(1, 1)
    w_in_b = w_in[l].astype(BF16)
    wkt_b = jnp.transpose(w_in_b[:, RWKV_COLS + 512:RWKV_COLS + 1024])
    w_mq_b, w_mk_b, w_mv_b, w_mo_b = (w[l].astype(BF16) for w in (w_mq, w_mk, w_mv, w_mo))
    wr_t = jnp.transpose(w_router[l])
    eb = e_bias[l].reshape(N_EXPERTS, 1).astype(F32)
    wsg, wsu, wsd = w_s_gate[l].astype(BF16), w_s_up[l].astype(BF16), w_s_down[l].astype(BF16)
    g_in, b_in = _row(ln_in_g), _row(ln_in_b)
    g2, b2, g3, b3 = _row(ln2_g[l]), _row(ln2_b[l]), _row(ln3_g[l]), _row(ln3_b[l])

    def rwkv(p, prev, nseq, tlen, s0, tm, shift_in_kernel, tc):
        a, w, wr, b, k2, v, br, kr, g, bv = rwkv_prep(p, prev, prep_consts, tm, shift_in_kernel, tlen)
        if nseq * N_RWKV_HEADS == SCAN_BH and tlen % LANES == 0:
            jmaj = scan_relayout((a, w, wr, b, k2), nseq, tlen, LANES)
        else:
            jmaj = [_to_scan_j(z, nseq, tlen) for z in (a, w, wr, b, k2)]
        y, st = wkv_scan(*jmaj, _to_scan_t(v, nseq, tlen),
                         _to_scan_s(br, nseq, tlen), _to_scan_s(kr, nseq, tlen), s0, tc)
        return _from_scan_j(y, nseq, tlen), bv, g, _state_from_scan(st, nseq)

    def tail(h1, om, tm, tm_moe):
        h2 = matmul_res_ln(om, h1, w_mo_b, g2, b2, tm, "mem_out_ln2")
        return moe_ln(h2, wr_t, eb, wsg, wsu, wsd, w_e_gate[l], w_e_up[l], w_e_down[l], g3, b3, tm_moe)

    n_p = nb * t
    tm = 512
    hp, pp, qp, kp, vp = ln_proj(x_prompt.reshape(n_p, D_MODEL), g_in, b_in, w_in_b, wkt_b, tm, seq_len=t)
    s0_p = jnp.zeros((nb * N_RWKV_HEADS // SCAN_BH, SCAN_JLO, RWKV_HD, LANES), F32)
    y_p, bv_p, gt_p, wkv_p = rwkv(pp, None, nb, t, s0_p, tm, True, 128)
    od_p = diff_attn_prompt(qp, kp, vp, rel_bias, lam, nb, t, 512)
    h1_p = mix_ln(y_p, bv_p, gt_p, od_p, hp, mix_consts, tm)
    mem_rows = mem_prompt.reshape(nb * N_MEM, D_MODEL)
    mk_p = matmul(mem_rows, w_mk_b, tm, "mem_k")
    mv_p = matmul(mem_rows, w_mv_b, tm, "mem_v")
    qm_p = matmul(h1_p, w_mq_b, tm, "mem_q").reshape(nb, t, D_MODEL)
    om_p = mem_attn(qm_p, mk_p.reshape(nb, N_MEM, D_MODEL), mv_p.reshape(nb, N_MEM, D_MODEL), tm)
    h3_p = tail(h1_p, om_p.reshape(n_p, D_MODEL), tm, 1024)

    n_s = db * s_len
    hs, ps, qs, ks, vs = ln_proj(x_sample.reshape(n_s, D_MODEL), g_in, b_in, w_in_b, wkt_b, n_s)
    ps3 = ps.reshape(db, s_len, RWKV_COLS)
    prev_s = jnp.concatenate([state_shift[l][:, None, :], ps3[:, :-1]], axis=1).reshape(n_s, RWKV_COLS)
    y_s, bv_s, gt_s, wkv_s = rwkv(ps, prev_s, db, s_len, _state_to_scan(state_wkv[l], db), n_s, False, s_len)
    n_pool = cache_k.shape[1]
    od_s = diff_attn_sample(qs.reshape(db, s_len, 512), ks.reshape(db, s_len, 512), vs.reshape(db, s_len, 512),
                            jnp.transpose(cache_k[l], (0, 2, 3, 4, 1)).reshape(n_pool, DIFF_WIDTH, PAGE_SIZE),
                            cache_v[l].reshape(n_pool, PAGE_SIZE * N_DIFF_HEADS, DIFF_VD),
                            page_table, rel_bias, lam, 32)
    h1_s = mix_ln(y_s, bv_s, gt_s, od_s.reshape(n_s, DIFF_WIDTH), hs, mix_consts, n_s)
    qm_s = matmul(h1_s, w_mq_b, n_s, "mem_q").reshape(db, s_len, D_MODEL)
    qm_s = jnp.pad(qm_s, ((0, 0), (0, SUBLANES - s_len), (0, 0)))
    om_s = mem_attn_cached(qm_s, _mem_cache_rows(cache_mem_k[l]), _mem_cache_rows(cache_mem_v[l]))
    h3_s = tail(h1_s, om_s[:, :s_len].reshape(n_s, D_MODEL), n_s, n_s)

    return (h3_p.reshape(nb, t, D_MODEL), h3_s.reshape(db, s_len, D_MODEL),
            jnp.transpose(kp.reshape(1, nb, N_DIFF_HEADS, 2, DIFF_HD, t), (0, 1, 5, 2, 3, 4)),
            vp.reshape(1, nb, t, N_DIFF_HEADS, DIFF_VD),
            wkv_p[None], pp.reshape(nb, t, RWKV_COLS)[:, -1][None],
            mk_p.reshape(1, nb, N_MEM, N_MEM_HEADS, MEM_HD), mv_p.reshape(1, nb, N_MEM, N_MEM_HEADS, MEM_HD),
            ks.reshape(1, db, s_len, N_DIFF_HEADS, 2, DIFF_HD), vs.reshape(1, db, s_len, N_DIFF_HEADS, DIFF_VD),
            wkv_s[None], ps3[:, -1][None])
```

```python
import functools
import math

import jax
import jax.numpy as jnp
from jax import lax
from jax.experimental import pallas as pl
from jax.experimental.pallas import tpu as pltpu

F32 = jnp.float32
BF16 = jnp.bfloat16

D_MODEL = 1024
RWKV_WIDTH = 512
RWKV_HD = 64
N_RWKV_HEADS = 8
RWKV_COLS = 1792
GN_EPS = 64e-5
DIFF_WIDTH = 512
DIFF_HD = 64
DIFF_VD = 128
N_DIFF_HEADS = 4
RMS_EPS = 1e-5
N_BUCKETS = 32
MAX_DISTANCE = 128
PAGE_SIZE = 128
N_MEM = 256
N_MEM_HEADS = 4
MEM_HD = 256
N_EXPERTS = 64
N_GROUPS = 8
GROUP_SIZE = 8
TOPK_GROUPS = 4
TOP_K = 8
D_EXPERT = 256
ROUTED_SCALE = 2.5
MOE_EXPERTS_PER_STEP = 4
DEPTH = 1
DN_ALPHA = (2.0 * DEPTH) ** 0.25
LN_EPS = 1e-5
NEG_INF = -1e30
LAM_INIT = 0.8 - 0.6 * math.exp(-0.3 * 0)

LANES = 128
SUBLANES = 8
VMEM_LIMIT = 60 * 1024 * 1024

SCAN_JLO = 16
SCAN_JHI = RWKV_HD // SCAN_JLO
SCAN_BH = LANES // SCAN_JHI


def _cparams(sem):
    return pltpu.CompilerParams(dimension_semantics=sem, vmem_limit_bytes=VMEM_LIMIT)


def _layer_norm(x, g, b):
    mu = jnp.mean(x, -1, keepdims=True)
    d = x - mu
    var = jnp.mean(d * d, -1, keepdims=True)
    return d * lax.rsqrt(var + LN_EPS) * g + b


def _dot(a, b):
    return jnp.dot(a, b, preferred_element_type=F32)


def _dot_nt(a, b):
    return lax.dot_general(a, b, (((1,), (1,)), ((), ())), preferred_element_type=F32)


def _seg_sum(x, ones_blk):
    outs = []
    for c in range(x.shape[1] // LANES):
        xc = x[:, c * LANES:(c + 1) * LANES]
        hi = xc.astype(BF16)
        lo = (xc - hi.astype(F32)).astype(BF16)
        outs.append(_dot(hi, ones_blk) + _dot(lo, ones_blk))
    return outs[0] if len(outs) == 1 else jnp.concatenate(outs, axis=1)


def _ln_proj_body(k_transposed, x_ref, g_ref, b_ref, w_ref, wkt_ref, h_ref, p_ref, q_ref, k_ref, v_ref):
    h = _layer_norm(x_ref[...], g_ref[...], b_ref[...])
    h_ref[...] = h
    hb = h.astype(BF16)
    c0 = RWKV_COLS
    p_ref[...] = _dot(hb, w_ref[:, 0:c0])
    q_ref[...] = _dot(hb, w_ref[:, c0:c0 + 512])
    if k_transposed:
        k_ref[0] = _dot_nt(wkt_ref[...], hb)
    else:
        k_ref[...] = _dot(hb, w_ref[:, c0 + 512:c0 + 1024])
    v_ref[...] = _dot(hb, w_ref[:, c0 + 1024:c0 + 1536])


def ln_proj(x, g, b, w_bf16, wkt_bf16, tm, seq_len=None):
    n = x.shape[0]
    row = lambda width: pl.BlockSpec((tm, width), lambda i: (i, 0))
    full = lambda a: pl.BlockSpec(a.shape, lambda i: (0,) * a.ndim)
    if seq_len is None:
        k_spec, k_shape = row(512), (n, 512)
    else:
        bps = seq_len // tm
        k_spec = pl.BlockSpec((1, 512, tm), lambda i: (i // bps, 0, i % bps))
        k_shape = (n // seq_len, 512, seq_len)
    shapes = [(n, D_MODEL), (n, RWKV_COLS), (n, 512), k_shape, (n, 512)]
    return pl.pallas_call(
        functools.partial(_ln_proj_body, seq_len is not None),
        grid=(n // tm,),
        in_specs=[row(D_MODEL), full(g), full(b), full(w_bf16), full(wkt_bf16)],
        out_specs=[row(D_MODEL), row(RWKV_COLS), row(512), k_spec, row(512)],
        out_shape=[jax.ShapeDtypeStruct(sh, F32) for sh in shapes],
        compiler_params=_cparams(("arbitrary",)),
        name="ln_proj",
    )(x, g, b, w_bf16, wkt_bf16)


def _mm_body(x_ref, w_ref, o_ref):
    o_ref[...] = _dot(x_ref[...].astype(BF16), w_ref[...])


def matmul(x, w_bf16, tm, name):
    n, kdim = x.shape
    nout = w_bf16.shape[1]
    return pl.pallas_call(
        _mm_body,
        grid=(n // tm,),
        in_specs=[pl.BlockSpec((tm, kdim), lambda i: (i, 0)), pl.BlockSpec((kdim, nout), lambda i: (0, 0))],
        out_specs=pl.BlockSpec((tm, nout), lambda i: (i, 0)),
        out_shape=jax.ShapeDtypeStruct((n, nout), F32),
        compiler_params=_cparams(("arbitrary",)),
        name=name,
    )(x, w_bf16)


def _mm_res_ln_body(x_ref, res_ref, w_ref, g_ref, b_ref, o_ref):
    y = _dot(x_ref[...].astype(BF16), w_ref[...])
    o_ref[...] = _layer_norm(DN_ALPHA * res_ref[...] + y, g_ref[...], b_ref[...])


def matmul_res_ln(x, res, w_bf16, g, b, tm, name):
    n, kdim = x.shape
    nout = w_bf16.shape[1]
    row = lambda width: pl.BlockSpec((tm, width), lambda i: (i, 0))
    full = lambda a: pl.BlockSpec(a.shape, lambda i: (0,) * a.ndim)
    return pl.pallas_call(
        _mm_res_ln_body,
        grid=(n // tm,),
        in_specs=[row(kdim), row(nout), full(w_bf16), full(g), full(b)],
        out_specs=row(nout),
        out_shape=jax.ShapeDtypeStruct((n, nout), F32),
        compiler_params=_cparams(("arbitrary",)),
        name=name,
    )(x, res, w_bf16, g, b)


def _rwkv_prep_math(p, p_prev, mu_ref, wlo_ref, w0a0_ref, wg_ref, kk_ref, ka_ref, rk_ref, seg_ref):
    ps = p + (p_prev - p) * mu_ref[...]
    r = ps[:, 0:512]
    k = ps[:, 512:1024]
    v = ps[:, 1024:1536]
    lo = ps[:, 1536:1664]
    g_lo = ps[:, 1664:1792]
    lane = lax.broadcasted_iota(jnp.int32, lo.shape, 1)
    lo = jnp.where(lane < 64, jnp.tanh(lo), lo)
    wa = _dot(lo.astype(BF16), wlo_ref[...]) + w0a0_ref[...]
    x = -wa[:, 0:512]
    softplus = jnp.maximum(x, 0.0) + jnp.log(1.0 + jnp.exp(-jnp.abs(x)))
    decay = jnp.exp(-jnp.exp(-softplus - 0.5))
    a = jax.nn.sigmoid(wa[:, 512:1024])
    g = _dot(jax.nn.sigmoid(g_lo).astype(BF16), wg_ref[...])
    seg = seg_ref[...]
    kk = k * kk_ref[...]
    kk = kk / jnp.maximum(jnp.sqrt(_seg_sum(kk * kk, seg)), 1e-12)
    k2 = k * (1.0 + (a - 1.0) * ka_ref[...])
    b = kk * a
    return (-kk, decay, decay * r, b, k2, v, _seg_sum(b * r, seg), _seg_sum(k2 * r, seg), g,
            _seg_sum(r * k2 * rk_ref[...], seg) * v)


def _rwkv_prep_body(p_ref, prev_ref, *rest):
    consts, outs = rest[:8], rest[8:]
    for o_ref, val in zip(outs, _rwkv_prep_math(p_ref[...], prev_ref[...], *consts)):
        o_ref[...] = val


def rwkv_prep(p, prev, consts, tm):
    n = p.shape[0]
    row = lambda width: pl.BlockSpec((tm, width), lambda i: (i, 0))
    full = lambda a: pl.BlockSpec(a.shape, lambda i: (0,) * a.ndim)
    return pl.pallas_call(
        _rwkv_prep_body,
        grid=(n // tm,),
        in_specs=[row(RWKV_COLS), row(RWKV_COLS)] + [full(c) for c in consts],
        out_specs=[row(512)] * 10,
        out_shape=[jax.ShapeDtypeStruct((n, 512), F32)] * 10,
        compiler_params=_cparams(("arbitrary",)),
        name="rwkv_prep",
    )(p, prev, *consts)


RELAYOUT_PITCH = RWKV_HD + SUBLANES


def _to_scan_layout(x, nb, tt, o_ref, scr):
    for b in range(nb):
        xt = jnp.transpose(x[b * tt:(b + 1) * tt, :])
        for h in range(N_RWKV_HEADS):
            scr[b, h * RELAYOUT_PITCH:h * RELAYOUT_PITCH + RWKV_HD, :] = xt[h * RWKV_HD:(h + 1) * RWKV_HD, :]
    for j in range(SCAN_JLO):
        rows = [scr[b, pl.ds(jq * SCAN_JLO + j, N_RWKV_HEADS, stride=RELAYOUT_PITCH), :]
                for jq in range(SCAN_JHI) for b in range(nb)]
        o_ref[0, j] = jnp.transpose(jnp.concatenate(rows, axis=0))


def _rwkv_prep_scan_body(nb, tt, p_ref, prev_ref, *rest):
    consts, outs, scr = rest[:8], rest[8:18], rest[18]
    p = p_ref[...].reshape(nb * tt, RWKV_COLS)
    row = lax.broadcasted_iota(jnp.int32, p.shape, 0)
    p_prev = pltpu.roll(p, 1, 0)
    for b in range(nb):
        first = jnp.where(pl.program_id(0) == 0, 0.0, prev_ref[b, SUBLANES - 1:SUBLANES, :])
        p_prev = jnp.where(row == b * tt, first, p_prev)
    vals = _rwkv_prep_math(p, p_prev, *consts)
    for o_ref, val in zip(outs[:5], vals[:5]):
        _to_scan_layout(val, nb, tt, o_ref, scr)
    for o_ref, val in zip(outs[5:], vals[5:]):
        o_ref[...] = val.reshape(nb, tt, RWKV_WIDTH)


def rwkv_prep_scan(p, consts, nb, t, tt):
    assert nb * N_RWKV_HEADS == SCAN_BH and t % tt == 0 and tt % LANES == 0
    p3 = p.reshape(nb, t, RWKV_COLS)
    per8 = tt // SUBLANES
    full = lambda a: pl.BlockSpec(a.shape, lambda i: (0,) * a.ndim)
    nat = pl.BlockSpec((nb, tt, RWKV_WIDTH), lambda i: (0, i, 0))
    scn = pl.BlockSpec((1, SCAN_JLO, tt, LANES), lambda i: (0, 0, i, 0))
    outs = pl.pallas_call(
        functools.partial(_rwkv_prep_scan_body, nb, tt),
        grid=(t // tt,),
        in_specs=[pl.BlockSpec((nb, tt, RWKV_COLS), lambda i: (0, i, 0)),
                  pl.BlockSpec((nb, SUBLANES, RWKV_COLS), lambda i: (0, jnp.maximum(i * per8 - 1, 0), 0))]
                 + [full(c) for c in consts],
        out_specs=[scn] * 5 + [nat] * 5,
        out_shape=[jax.ShapeDtypeStruct((1, SCAN_JLO, t, LANES), F32)] * 5
                  + [jax.ShapeDtypeStruct((nb, t, RWKV_WIDTH), F32)] * 5,
        scratch_shapes=[pltpu.VMEM((nb, N_RWKV_HEADS * RELAYOUT_PITCH, tt), F32)],
        compiler_params=_cparams(("arbitrary",)),
        name="rwkv_prep_scan",
    )(p3, p3, *consts)
    return list(outs[:5]) + [o.reshape(nb * t, RWKV_WIDTH) for o in outs[5:]]


def _lane_fold(x):
    return x + pltpu.roll(x, 32, 1) + pltpu.roll(x, 64, 1) + pltpu.roll(x, 96, 1)


def _wkv_scan_body(tc, a_ref, w_ref, wr_ref, b_ref, k_ref, v_ref, br_ref, kr_ref, s0_ref,
                   y_ref, st_ref, s_scr):
    c = pl.program_id(1)
    ngrp = RWKV_HD // SUBLANES

    @pl.when(c == 0)
    def _():
        s_scr[...] = s0_ref[0]

    def bcast(ref, t, j):
        return jnp.broadcast_to(ref[0, j, pl.ds(t, 1), :], (SUBLANES, LANES))

    lane_grp = lax.broadcasted_iota(jnp.int32, (SUBLANES, LANES), 1) // SCAN_BH

    def step(t, carry):
        acc_a = [jnp.zeros((SUBLANES, LANES), F32) for _ in range(ngrp)]
        acc_y = [jnp.zeros((SUBLANES, LANES), F32) for _ in range(ngrp)]
        for j in range(SCAN_JLO):
            ab = bcast(a_ref, t, j)
            wrb = bcast(wr_ref, t, j)
            for ig in range(ngrp):
                s = s_scr[j, ig * SUBLANES:(ig + 1) * SUBLANES, :]
                acc_a[ig] = acc_a[ig] + s * ab
                acc_y[ig] = acc_y[ig] + s * wrb
        sa = [_lane_fold(x) for x in acc_a]
        vp = [v_ref[0, t, m * SUBLANES:(m + 1) * SUBLANES, :] for m in range(2)]
        vrot = [[x] + [pltpu.roll(x, SCAN_BH * k, 1) for k in range(1, SCAN_JHI)] for x in vp]
        vv = []
        for ig in range(ngrp):
            q, rot = ig // 2, vrot[ig % 2]
            x = rot[(SCAN_JHI - 1 - q) % SCAN_JHI]
            for g in range(SCAN_JHI - 2, -1, -1):
                x = jnp.where(lane_grp == g, rot[(g - q) % SCAN_JHI], x)
            vv.append(x)
        for j in range(SCAN_JLO):
            wb = bcast(w_ref, t, j)
            bb = bcast(b_ref, t, j)
            kb = bcast(k_ref, t, j)
            for ig in range(ngrp):
                sl = slice(ig * SUBLANES, (ig + 1) * SUBLANES)
                s_scr[j, sl, :] = s_scr[j, sl, :] * wb + sa[ig] * bb + vv[ig] * kb
        br = jnp.broadcast_to(br_ref[0, t], (SUBLANES, LANES))
        kr = jnp.broadcast_to(kr_ref[0, t], (SUBLANES, LANES))
        y = [_lane_fold(acc_y[ig]) + sa[ig] * br + vv[ig] * kr for ig in range(ngrp)]
        for m in range(2):
            yp = y[6 + m]
            for q in (2, 1, 0):
                yp = jnp.where(lane_grp == q, y[2 * q + m], yp)
            y_ref[0, t, m * SUBLANES:(m + 1) * SUBLANES, :] = yp
        return carry

    lax.fori_loop(0, tc, step, 0)

    @pl.when(c == pl.num_programs(1) - 1)
    def _():
        st_ref[0] = s_scr[...]


def wkv_scan(a, w, wr, b, k, v, br, kr, s0, tc):
    g, t = v.shape[0], v.shape[1]
    jspec = pl.BlockSpec((1, SCAN_JLO, tc, LANES), lambda gi, c: (gi, 0, c, 0))
    tspec = pl.BlockSpec((1, tc, SCAN_JLO, LANES), lambda gi, c: (gi, c, 0, 0))
    sspec = pl.BlockSpec((1, tc, 1, LANES), lambda gi, c: (gi, c, 0, 0))
    stspec = pl.BlockSpec((1, SCAN_JLO, RWKV_HD, LANES), lambda gi, c: (gi, 0, 0, 0))
    return pl.pallas_call(
        functools.partial(_wkv_scan_body, tc),
        grid=(g, t // tc),
        in_specs=[jspec] * 5 + [tspec, sspec, sspec, stspec],
        out_specs=[tspec, stspec],
        out_shape=[jax.ShapeDtypeStruct((g, t, SCAN_JLO, LANES), F32),
                   jax.ShapeDtypeStruct((g, SCAN_JLO, RWKV_HD, LANES), F32)],
        scratch_shapes=[pltpu.VMEM((SCAN_JLO, RWKV_HD, LANES), F32)],
        compiler_params=_cparams(("arbitrary", "arbitrary")),
        name="wkv_scan",
    )(a, w, wr, b, k, v, br, kr, s0)


def _to_scan_t(x, nb, t):
    g = nb * N_RWKV_HEADS // SCAN_BH
    bp = nb // g
    x = x.reshape(g, bp, t, N_RWKV_HEADS, SCAN_JHI, SCAN_JLO)
    return jnp.transpose(x, (0, 2, 5, 4, 1, 3)).reshape(g, t, SCAN_JLO, LANES)


def _to_scan_j(x, nb, t):
    g = nb * N_RWKV_HEADS // SCAN_BH
    bp = nb // g
    x = x.reshape(g, bp, t, N_RWKV_HEADS, SCAN_JHI, SCAN_JLO)
    return jnp.transpose(x, (0, 5, 2, 4, 1, 3)).reshape(g, SCAN_JLO, t, LANES)


def _from_scan_j(x, nb, t):
    g = x.shape[0]
    bp = nb // g
    x = x.reshape(g, t, SCAN_JLO, SCAN_JHI, bp, N_RWKV_HEADS)
    return jnp.transpose(x, (0, 4, 1, 5, 3, 2)).reshape(nb * t, RWKV_WIDTH)


def _to_scan_s(x, nb, t):
    g = nb * N_RWKV_HEADS // SCAN_BH
    bp = nb // g
    x = x.reshape(g, bp, t, N_RWKV_HEADS, RWKV_HD)[..., 0]
    x = jnp.transpose(x, (0, 2, 1, 3)).reshape(g, t, 1, 1, SCAN_BH)
    return jnp.broadcast_to(x, (g, t, 1, SCAN_JHI, SCAN_BH)).reshape(g, t, 1, LANES)


def _state_to_scan(s, nb):
    g = nb * N_RWKV_HEADS // SCAN_BH
    bp = nb // g
    s = s.reshape(g, bp, N_RWKV_HEADS, RWKV_HD, SCAN_JHI, SCAN_JLO)
    return jnp.transpose(s, (0, 5, 3, 4, 1, 2)).reshape(g, SCAN_JLO, RWKV_HD, LANES)


def _state_from_scan(s, nb):
    g = s.shape[0]
    bp = nb // g
    s = s.reshape(g, SCAN_JLO, RWKV_HD, SCAN_JHI, bp, N_RWKV_HEADS)
    return jnp.transpose(s, (0, 4, 5, 2, 3, 1)).reshape(nb, N_RWKV_HEADS, RWKV_HD, RWKV_HD)


def _mix_body(y_ref, bv_ref, g_ref, od_ref, h_ref, lnxg_ref, lnxb_ref, sub_ref, wout_ref, g1_ref, b1_ref,
              seg_ref, ones_ref, o_ref):
    seg = seg_ref[...]
    y = y_ref[...]
    mu = _seg_sum(y, seg) * (1.0 / RWKV_HD)
    d = y - mu
    var = _seg_sum(d * d, seg) * (1.0 / RWKV_HD)
    yn = d * lax.rsqrt(var + GN_EPS) * lnxg_ref[...] + lnxb_ref[...]
    o_rw = (yn + bv_ref[...]) * g_ref[...]
    od = od_ref[...]
    ms = _seg_sum(od * od, ones_ref[...]) * (1.0 / DIFF_VD)
    o_df = od * lax.rsqrt(ms + RMS_EPS) * sub_ref[...]
    mix = _dot(o_rw.astype(BF16), wout_ref[0:512, :]) + _dot(o_df.astype(BF16), wout_ref[512:1024, :])
    o_ref[...] = _layer_norm(DN_ALPHA * h_ref[...] + mix, g1_ref[...], b1_ref[...])


def mix_ln(y, bv, g, od, h, consts, tm):
    n = y.shape[0]
    row = lambda width: pl.BlockSpec((tm, width), lambda i: (i, 0))
    full = lambda a: pl.BlockSpec(a.shape, lambda i: (0,) * a.ndim)
    return pl.pallas_call(
        _mix_body,
        grid=(n // tm,),
        in_specs=[row(512)] * 4 + [row(D_MODEL)] + [full(c) for c in consts],
        out_specs=row(D_MODEL),
        out_shape=jax.ShapeDtypeStruct((n, D_MODEL), F32),
        compiler_params=_cparams(("arbitrary",)),
        name="mix_ln1",
    )(y, bv, g, od, h, *consts)


def _t5_bucket(dist):
    max_exact = N_BUCKETS // 2
    d = jnp.maximum(dist, 1).astype(F32)
    large = max_exact + (jnp.log(d / max_exact) / math.log(MAX_DISTANCE / max_exact)
                         * (N_BUCKETS - max_exact)).astype(jnp.int32)
    return jnp.where(dist < max_exact, dist, jnp.minimum(large, N_BUCKETS - 1))


def _bias_table(rel_bias, dist):
    bucket = _t5_bucket(jnp.maximum(dist, 0))
    tail = (1,) * dist.ndim
    bias = jnp.zeros(rel_bias.shape[1:] + dist.shape, F32)
    for bk in range(N_BUCKETS):
        bias = jnp.where(bucket == bk, rel_bias[bk].astype(F32).reshape(rel_bias.shape[1:] + tail), bias)
    return jnp.where(dist >= 0, bias, NEG_INF)


def _bias_tiles_body(tq, tk, thr_ref, val_ref, o_ref):
    h = pl.program_id(0)
    rr = lax.broadcasted_iota(jnp.int32, (tq, tk), 0)
    cc = lax.broadcasted_iota(jnp.int32, (tq, tk), 1)
    for kind, base in enumerate((tq, 0)):
        d = base + rr - cc
        for m in range(2):
            bias = jnp.zeros((tq, tk), F32) + val_ref[h * 2 + m]
            for bk in range(1, N_BUCKETS):
                bias = jnp.where(d >= thr_ref[bk], val_ref[(bk * N_DIFF_HEADS + h) * 2 + m], bias)
            o_ref[0, kind, m * tq:(m + 1) * tq, :] = jnp.where(d >= 0, bias, NEG_INF)


def bias_tiles(rel_bias, tq, tk):
    dd = jnp.arange(MAX_DISTANCE + 1, dtype=jnp.int32)
    bucket = _t5_bucket(dd)
    thr = jnp.sum(bucket[None, :] < jnp.arange(N_BUCKETS, dtype=jnp.int32)[:, None], axis=1).astype(jnp.int32)
    smem = pl.BlockSpec(memory_space=pltpu.SMEM)
    return pl.pallas_call(
        functools.partial(_bias_tiles_body, tq, tk),
        grid=(N_DIFF_HEADS,),
        in_specs=[smem, smem],
        out_specs=pl.BlockSpec((1, 2, 2 * tq, tk), lambda h: (h, 0, 0, 0)),
        out_shape=jax.ShapeDtypeStruct((N_DIFF_HEADS, 2, 2 * tq, tk), F32),
        compiler_params=_cparams(("arbitrary",)),
        name="bias_tiles",
    )(thr, rel_bias.astype(F32).reshape(-1))


def _dap_body(tq, tk, qi_ref, ki_ref, lam_ref, q_ref, k_ref, v_ref, bias_ref, cfar_ref, o_ref,
              qs_scr, m_scr, l_scr, acc_scr):
    step = pl.program_id(2)
    qi = qi_ref[step]
    ki = ki_ref[step]

    @pl.when(ki == 0)
    def _():
        q = q_ref[...] * (DIFF_HD ** -0.5)
        lane = lax.broadcasted_iota(jnp.int32, q.shape, 1)
        qs_scr[0:tq, :] = jnp.where(lane < DIFF_HD, q, 0.0).astype(BF16)
        qs_scr[tq:2 * tq, :] = jnp.where(lane >= DIFF_HD, q, 0.0).astype(BF16)
        m_scr[...] = jnp.full(m_scr.shape, NEG_INF, F32)
        l_scr[...] = jnp.zeros(l_scr.shape, F32)
        acc_scr[...] = jnp.zeros(acc_scr.shape, F32)

    def update(s, shift):
        m_prev = m_scr[...]
        m_new = jnp.maximum(m_prev, jnp.max(s, axis=1, keepdims=True) + shift)
        alpha = jnp.exp(m_prev - m_new)
        p = jnp.exp(s - jnp.tile(m_new - shift, (1, tk // LANES)))
        l_scr[...] = alpha * l_scr[...] + jnp.sum(p, axis=1, keepdims=True)
        acc_scr[...] = alpha * acc_scr[...] + _dot(p.astype(BF16), v_ref[...].astype(BF16))
        m_scr[...] = m_new

    @pl.when(ki < qi - 1)
    def _():
        update(_dot(qs_scr[...], k_ref[0].astype(BF16)), cfar_ref[0])

    @pl.when(ki >= qi - 1)
    def _():
        update(_dot(qs_scr[...], k_ref[0].astype(BF16)) + bias_ref[0, 0], jnp.zeros((2 * tq, LANES), F32))

    @pl.when(ki == qi)
    def _():
        o = acc_scr[...] / l_scr[...]
        o_ref[...] = o[0:tq] - lam_ref[0, 0] * o[tq:2 * tq]


def diff_attn_prompt(q, kt, v, rel_bias, lam, nb, t, tq):
    tk = tq
    assert tq >= MAX_DISTANCE and t % tq == 0
    nq = t // tq
    qi_list, ki_list = [], []
    for a in range(nq):
        for c in range(a + 1):
            qi_list.append(a)
            ki_list.append(c)
    qi_arr = jnp.asarray(qi_list, jnp.int32)
    ki_arr = jnp.asarray(ki_list, jnp.int32)
    tiles = bias_tiles(rel_bias, tq, tk)
    cfar = jnp.broadcast_to(rel_bias[N_BUCKETS - 1].astype(F32)[:, :, None, None], (N_DIFF_HEADS, 2, tq, LANES))
    cfar = cfar.reshape(N_DIFF_HEADS, 2 * tq, LANES)

    grid_spec = pltpu.PrefetchScalarGridSpec(
        num_scalar_prefetch=2,
        grid=(nb, N_DIFF_HEADS, len(qi_list)),
        in_specs=[
            pl.BlockSpec(memory_space=pltpu.SMEM),
            pl.BlockSpec((tq, LANES), lambda b, h, s, qi, ki: (b * nq + qi[s], h)),
            pl.BlockSpec((1, LANES, tk), lambda b, h, s, qi, ki: (b, h, ki[s])),
            pl.BlockSpec((tk, LANES), lambda b, h, s, qi, ki: (b * nq + ki[s], h)),
            pl.BlockSpec((1, 1, 2 * tq, tk), lambda b, h, s, qi, ki: (h, jnp.where(ki[s] == qi[s], 1, 0), 0, 0)),
            pl.BlockSpec((1, 2 * tq, LANES), lambda b, h, s, qi, ki: (h, 0, 0)),
        ],
        out_specs=pl.BlockSpec((tq, LANES), lambda b, h, s, qi, ki: (b * nq + qi[s], h)),
        scratch_shapes=[pltpu.VMEM((2 * tq, LANES), BF16), pltpu.VMEM((2 * tq, LANES), F32),
                        pltpu.VMEM((2 * tq, LANES), F32), pltpu.VMEM((2 * tq, LANES), F32)],
    )
    return pl.pallas_call(
        functools.partial(_dap_body, tq, tk),
        grid_spec=grid_spec,
        out_shape=jax.ShapeDtypeStruct((nb * t, DIFF_WIDTH), F32),
        compiler_params=_cparams(("arbitrary", "arbitrary", "arbitrary")),
        name="diff_attn_prompt",
    )(qi_arr, ki_arr, lam, q, kt, v, tiles, cfar)


def _das_body(npg, s_len, pt_ref, lam_ref, q_ref, qmask_ref, bias_ref, bnew_ref, kn_ref, vn_ref, *rest):
    k_refs = rest[0:npg]
    v_refs = rest[npg:2 * npg]
    o_ref = rest[2 * npg]
    qs_scr, m_scr, l_scr, acc_scr = rest[2 * npg + 1:]
    g = pl.program_id(1)
    rows = 2 * N_DIFF_HEADS * s_len

    @pl.when(g == 0)
    def _():
        q = q_ref[0] * (DIFF_HD ** -0.5)
        qrep = jnp.concatenate([q] * (2 * N_DIFF_HEADS), axis=0)
        qs_scr[...] = (qrep * qmask_ref[...]).astype(BF16)
        m_scr[...] = jnp.full(m_scr.shape, NEG_INF, F32)
        l_scr[...] = jnp.zeros(l_scr.shape, F32)
        acc_scr[...] = jnp.zeros(acc_scr.shape, F32)

    hrows = 2 * s_len
    qs = qs_scr[...]
    s = jnp.concatenate([_dot(qs, kr[0].astype(BF16)) for kr in k_refs], axis=1) + bias_ref[0]
    m_prev = m_scr[...]
    m_new = jnp.maximum(m_prev, jnp.max(s, axis=1, keepdims=True))
    alpha = jnp.exp(m_prev - m_new)
    p = jnp.exp(s - jnp.tile(m_new, (1, npg)))
    l_scr[...] = alpha * l_scr[...] + jnp.sum(p, axis=1, keepdims=True)
    pvs = []
    for h in range(N_DIFF_HEADS):
        pv = None
        for j in range(npg):
            ph = p[h * hrows:(h + 1) * hrows, j * PAGE_SIZE:(j + 1) * PAGE_SIZE].astype(BF16)
            vh = v_refs[j][0, pl.ds(h, PAGE_SIZE, stride=N_DIFF_HEADS), :].astype(BF16)
            d = _dot(ph, vh)
            pv = d if pv is None else pv + d
        pvs.append(pv)
    acc_scr[...] = alpha * acc_scr[...] + jnp.concatenate(pvs, axis=0)
    m_scr[...] = m_new

    @pl.when(g == pl.num_programs(1) - 1)
    def _():
        qf = qs_scr[...].astype(F32)
        kn = kn_ref[0].astype(BF16).astype(F32)
        vn = vn_ref[0].astype(BF16).astype(F32)
        lane = lax.broadcasted_iota(jnp.int32, (rows, LANES), 1)
        sn = bnew_ref[...]
        for j in range(s_len):
            dj = jnp.sum(qf * kn[j:j + 1, :], axis=1, keepdims=True)
            sn = sn + jnp.where(lane == j, dj, 0.0)
        m_prev = m_scr[...]
        m_new = jnp.maximum(m_prev, jnp.max(sn, axis=1, keepdims=True))
        alpha = jnp.exp(m_prev - m_new)
        pn = jnp.exp(sn - m_new).astype(BF16).astype(F32)
        l_fin = alpha * l_scr[...] + jnp.sum(pn, axis=1, keepdims=True)
        acc = alpha * acc_scr[...]
        for j in range(s_len):
            vj = jnp.concatenate([jnp.broadcast_to(vn[j:j + 1, h * DIFF_VD:(h + 1) * DIFF_VD], (hrows, DIFF_VD))
                                  for h in range(N_DIFF_HEADS)], axis=0)
            acc = acc + pn[:, j:j + 1] * vj
        o = acc / l_fin
        lam = lam_ref[0, 0]
        outs = []
        for h in range(N_DIFF_HEADS):
            blk = o[h * hrows:(h + 1) * hrows, :]
            outs.append(blk[0:s_len] - lam * blk[s_len:hrows])
        o_ref[0] = jnp.concatenate(outs, axis=1)


def diff_attn_sample(q, k, v, cache_k, cache_v, page_table, rel_bias, lam, npg):
    db, s_len, _ = q.shape
    n_pages = page_table.shape[1]
    past = n_pages * PAGE_SIZE
    assert n_pages % npg == 0
    ngroups = n_pages // npg
    rows = 2 * N_DIFF_HEADS * s_len
    width = npg * PAGE_SIZE
    hm = jnp.arange(rows, dtype=jnp.int32) // s_len
    qmask = (jnp.arange(DIFF_WIDTH, dtype=jnp.int32)[None, :] // DIFF_HD == hm[:, None]).astype(F32)
    qpos = past + jnp.arange(s_len, dtype=jnp.int32)

    def rows_of(table):
        return table.reshape(rows, table.shape[-1])

    far = rows_of(_bias_table(rel_bias, jnp.full((s_len, width), 2 * MAX_DISTANCE, jnp.int32)))
    kpos_last = past - width + jnp.arange(width, dtype=jnp.int32)
    last = rows_of(_bias_table(rel_bias, qpos[:, None] - kpos_last[None, :]))
    bias = jnp.stack([far, last], axis=0)
    knew_pos = past + jnp.arange(LANES, dtype=jnp.int32)
    dist_new = jnp.where(jnp.arange(LANES)[None, :] < s_len, qpos[:, None] - knew_pos[None, :], -1)
    bnew = rows_of(_bias_table(rel_bias, dist_new))
    assert width >= MAX_DISTANCE + s_len
    pt_flat = page_table.reshape(-1).astype(jnp.int32)

    def page_spec(j):
        return pl.BlockSpec((1, DIFF_WIDTH, PAGE_SIZE),
                            lambda b, g, pt, j=j: (pt[b * n_pages + g * npg + j], 0, 0))

    seq_spec = pl.BlockSpec((1, s_len, DIFF_WIDTH), lambda b, g, pt: (b, 0, 0))
    full2 = lambda a: pl.BlockSpec(a.shape, lambda b, g, pt: (0,) * a.ndim)
    grid_spec = pltpu.PrefetchScalarGridSpec(
        num_scalar_prefetch=1,
        grid=(db, ngroups),
        in_specs=[pl.BlockSpec(memory_space=pltpu.SMEM), seq_spec, full2(qmask),
                  pl.BlockSpec((1, rows, width), lambda b, g, pt: (jnp.where(g == ngroups - 1, 1, 0), 0, 0)),
                  full2(bnew), seq_spec, seq_spec]
                 + [page_spec(j) for j in range(npg)] + [page_spec(j) for j in range(npg)],
        out_specs=seq_spec,
        scratch_shapes=[pltpu.VMEM((rows, DIFF_WIDTH), BF16), pltpu.VMEM((rows, LANES), F32),
                        pltpu.VMEM((rows, LANES), F32), pltpu.VMEM((rows, DIFF_VD), F32)],
    )
    return pl.pallas_call(
        functools.partial(_das_body, npg, s_len),
        grid_spec=grid_spec,
        out_shape=jax.ShapeDtypeStruct((db, s_len, DIFF_WIDTH), F32),
        compiler_params=_cparams(("arbitrary", "arbitrary")),
        name="diff_attn_sample",
    )(pt_flat, lam, q, qmask, bias, bnew, k, v, *([cache_k] * npg), *([cache_v] * npg))


def _mem_attn_body(q_ref, mk_ref, mv_ref, o_ref):
    for h in range(N_MEM_HEADS):
        cols = slice(h * MEM_HD, (h + 1) * MEM_HD)
        qh = (q_ref[0, :, cols] * (MEM_HD ** -0.5)).astype(BF16)
        s = _dot_nt(qh, mk_ref[0, :, cols].astype(BF16))
        p = jnp.exp(s - jnp.max(s, axis=1, keepdims=True))
        l = jnp.sum(p, axis=1, keepdims=True)
        o_ref[0, :, cols] = _dot(p.astype(BF16), mv_ref[0, :, cols].astype(BF16)) / l


def mem_attn(q, mk, mv, tm):
    nb, t, _ = q.shape
    qspec = pl.BlockSpec((1, tm, D_MODEL), lambda b, i: (b, i, 0))
    mspec = pl.BlockSpec((1, N_MEM, D_MODEL), lambda b, i: (b, 0, 0))
    return pl.pallas_call(
        _mem_attn_body,
        grid=(nb, t // tm),
        in_specs=[qspec, mspec, mspec],
        out_specs=qspec,
        out_shape=jax.ShapeDtypeStruct((nb, t, D_MODEL), F32),
        compiler_params=_cparams(("arbitrary", "arbitrary")),
        name="mem_attn",
    )(q, mk, mv)


MEM_COL_TILES = MEM_HD // LANES


def _mem_cache_rows(cache):
    nb = cache.shape[0]
    c = cache.reshape(nb, N_MEM, N_MEM_HEADS, MEM_COL_TILES, LANES)
    return jnp.transpose(c, (0, 1, 3, 2, 4)).reshape(nb, N_MEM * MEM_COL_TILES * N_MEM_HEADS, LANES)


def _mem_attn_cached_body(q_ref, mk_ref, mv_ref, o_ref):
    stride = MEM_COL_TILES * N_MEM_HEADS

    def tile(ref, h, ct):
        return ref[0, pl.ds(ct * N_MEM_HEADS + h, N_MEM, stride=stride), :].astype(BF16)

    for h in range(N_MEM_HEADS):
        s = None
        for ct in range(MEM_COL_TILES):
            c0 = h * MEM_HD + ct * LANES
            qh = (q_ref[0, :, c0:c0 + LANES] * (MEM_HD ** -0.5)).astype(BF16)
            d = _dot_nt(qh, tile(mk_ref, h, ct))
            s = d if s is None else s + d
        p = jnp.exp(s - jnp.max(s, axis=1, keepdims=True))
        l = jnp.sum(p, axis=1, keepdims=True)
        pb = p.astype(BF16)
        for ct in range(MEM_COL_TILES):
            c0 = h * MEM_HD + ct * LANES
            o_ref[0, :, c0:c0 + LANES] = _dot(pb, tile(mv_ref, h, ct)) / l


def mem_attn_cached(q, mk_rows, mv_rows):
    nb, t, _ = q.shape
    qspec = pl.BlockSpec((1, t, D_MODEL), lambda b: (b, 0, 0))
    mspec = pl.BlockSpec((1,) + mk_rows.shape[1:], lambda b: (b, 0, 0))
    return pl.pallas_call(
        _mem_attn_cached_body,
        grid=(nb,),
        in_specs=[qspec, mspec, mspec],
        out_specs=qspec,
        out_shape=jax.ShapeDtypeStruct((nb, t, D_MODEL), F32),
        compiler_params=_cparams(("arbitrary",)),
        name="mem_attn_cached",
    )(q, mk_rows, mv_rows)


def _route(x, wr_t, e_bias):
    tm = x.shape[0]
    logits = lax.dot_general(wr_t, x, (((1,), (1,)), ((), ())), precision=lax.Precision.HIGHEST,
                             preferred_element_type=F32)
    scores = jax.nn.sigmoid(logits)
    biased = scores + e_bias
    sub = lax.broadcasted_iota(jnp.int32, (GROUP_SIZE, tm), 0)
    grp_rows = []
    for gi in range(N_GROUPS):
        xg = biased[gi * GROUP_SIZE:(gi + 1) * GROUP_SIZE, :]
        m1 = jnp.max(xg, axis=0, keepdims=True)
        i1 = jnp.min(jnp.where(xg == m1, sub, GROUP_SIZE), axis=0, keepdims=True)
        m2 = jnp.max(jnp.where(sub == i1, -jnp.inf, xg), axis=0, keepdims=True)
        grp_rows.append(m1 + m2)
    grp = jnp.concatenate(grp_rows, axis=0)
    gidx = lax.broadcasted_iota(jnp.int32, (N_GROUPS, tm), 0)
    grank = jnp.zeros((N_GROUPS, tm), jnp.int32)
    for gi in range(N_GROUPS):
        other = jnp.broadcast_to(grp[gi:gi + 1, :], (N_GROUPS, tm))
        grank = grank + ((other > grp) | ((other == grp) & (gi < gidx))).astype(jnp.int32)
    gsel = grank < TOPK_GROUPS
    emask = jnp.concatenate([jnp.broadcast_to(gsel[gi:gi + 1, :], (GROUP_SIZE, tm)) for gi in range(N_GROUPS)],
                            axis=0)
    masked = jnp.where(emask, biased, NEG_INF)
    eidx = lax.broadcasted_iota(jnp.int32, (N_EXPERTS, tm), 0)
    erank = jnp.zeros((N_EXPERTS, tm), jnp.int32)
    for e in range(N_EXPERTS):
        other = jnp.broadcast_to(masked[e:e + 1, :], (N_EXPERTS, tm))
        erank = erank + ((other > masked) | ((other == masked) & (e < eidx))).astype(jnp.int32)
    sel = jnp.where(erank < TOP_K, scores, 0.0)
    gates = sel / jnp.sum(sel, axis=0, keepdims=True) * ROUTED_SCALE
    return jnp.concatenate([gates, jnp.zeros_like(gates)], axis=0)


def _moe_body(x_ref, wrt_ref, eb_ref, wsg_ref, wsu_ref, wsd_ref, wg_ref, wu_ref, wd_ref, g3_ref, b3_ref,
              o_ref, xb_scr, gate_scr, acc_scr):
    e = pl.program_id(1)

    @pl.when(e == 0)
    def _():
        x = x_ref[...]
        xb = x.astype(BF16)
        xb_scr[...] = xb
        gate_scr[...] = jnp.transpose(_route(x, wrt_ref[...], eb_ref[...]))
        hs = jax.nn.silu(_dot(xb, wsg_ref[...])) * _dot(xb, wsu_ref[...])
        acc_scr[...] = _dot(hs.astype(BF16), wsd_ref[...])

    xb = xb_scr[...]
    lane = lax.broadcasted_iota(jnp.int32, gate_scr.shape, 1)
    acc = acc_scr[...]
    for j in range(MOE_EXPERTS_PER_STEP):
        hg = _dot(xb, wg_ref[j].astype(BF16))
        hu = _dot(xb, wu_ref[j].astype(BF16))
        gcol = jnp.sum(jnp.where(lane == e * MOE_EXPERTS_PER_STEP + j, gate_scr[...], 0.0), axis=1, keepdims=True)
        act = jax.nn.silu(hg) * hu * gcol
        acc = acc + _dot(act.astype(BF16), wd_ref[j].astype(BF16))
    acc_scr[...] = acc

    @pl.when(e == pl.num_programs(1) - 1)
    def _():
        o_ref[...] = _layer_norm(DN_ALPHA * x_ref[...] + acc_scr[...], g3_ref[...], b3_ref[...])


def moe_ln(x, wr_t, e_bias, wsg, wsu, wsd, w_e_gate, w_e_up, w_e_down, g3, b3, tm):
    n = x.shape[0]
    eps = MOE_EXPERTS_PER_STEP
    row = pl.BlockSpec((tm, D_MODEL), lambda i, e: (i, 0))
    full = lambda a: pl.BlockSpec(a.shape, lambda i, e: (0,) * a.ndim)
    return pl.pallas_call(
        _moe_body,
        grid=(n // tm, N_EXPERTS // eps),
        in_specs=[row, full(wr_t), full(e_bias), full(wsg), full(wsu), full(wsd),
                  pl.BlockSpec((eps, D_MODEL, D_EXPERT), lambda i, e: (e, 0, 0)),
                  pl.BlockSpec((eps, D_MODEL, D_EXPERT), lambda i, e: (e, 0, 0)),
                  pl.BlockSpec((eps, D_EXPERT, D_MODEL), lambda i, e: (e, 0, 0)),
                  full(g3), full(b3)],
        out_specs=row,
        out_shape=jax.ShapeDtypeStruct((n, D_MODEL), F32),
        scratch_shapes=[pltpu.VMEM((tm, D_MODEL), BF16), pltpu.VMEM((tm, LANES), F32),
                        pltpu.VMEM((tm, D_MODEL), F32)],
        compiler_params=_cparams(("arbitrary", "arbitrary")),
        name="moe_ln3",
    )(x, wr_t, e_bias, wsg, wsu, wsd, w_e_gate, w_e_up, w_e_down, g3, b3)


def _block_ones(seg):
    idx = jnp.arange(LANES, dtype=jnp.int32) // seg
    return (idx[:, None] == idx[None, :]).astype(BF16)


def _row(v):
    return v.reshape(1, -1).astype(F32)


def kernel(x_prompt, x_sample, cache_k, cache_v, state_wkv, state_shift, cache_mem_k, cache_mem_v, page_table, mem_prompt, ln_in_g, ln_in_b, rel_bias, w_in, mu_shift, w0, w_decay_up, a0, w_aaa_up, w_gate_up, k_k, k_a, r_k, lnx_g, lnx_b, lam_q1, lam_k1, lam_q2, lam_k2, subln_g, w_out, ln1_g, ln1_b, w_mq, w_mk, w_mv, w_mo, ln2_g, ln2_b, w_router, e_bias, w_e_gate, w_e_up, w_e_down, w_s_gate, w_s_up, w_s_down, ln3_g, ln3_b):
    assert w_in.shape[0] == DEPTH == 1
    nb, t, _ = x_prompt.shape
    db, s_len, _ = x_sample.shape
    l = 0
    seg64 = _block_ones(RWKV_HD)
    ones128 = _block_ones(LANES)
    zeros = jnp.zeros((64, 512), F32)
    w_lo = jnp.concatenate([jnp.concatenate([w_decay_up[l], zeros], axis=1),
                            jnp.concatenate([zeros, w_aaa_up[l]], axis=1)], axis=0).astype(BF16)
    prep_consts = (_row(mu_shift[l]), w_lo, _row(jnp.concatenate([w0[l], a0[l]])), w_gate_up[l].astype(BF16),
                   _row(k_k[l]), _row(k_a[l]), _row(r_k[l]), seg64)
    mix_consts = (_row(lnx_g[l]), _row(lnx_b[l]), _row(jnp.tile(subln_g[l] * (1.0 - LAM_INIT), N_DIFF_HEADS)),
                  w_out[l].astype(BF16), _row(ln1_g[l]), _row(ln1_b[l]), seg64, ones128)
    lam = (jnp.exp(jnp.sum(lam_q1[l] * lam_k1[l]).astype(F32)) - jnp.exp(jnp.sum(lam_q2[l] * lam_k2[l]).astype(F32))
           + LAM_INIT).reshape(1, 1)
    w_in_b = w_in[l].astype(BF16)
    wkt_b = jnp.transpose(w_in_b[:, RWKV_COLS + 512:RWKV_COLS + 1024])
    w_mq_b, w_mk_b, w_mv_b, w_mo_b = (w[l].astype(BF16) for w in (w_mq, w_mk, w_mv, w_mo))
    wr_t = jnp.transpose(w_router[l])
    eb = e_bias[l].reshape(N_EXPERTS, 1).astype(F32)
    wsg, wsu, wsd = w_s_gate[l].astype(BF16), w_s_up[l].astype(BF16), w_s_down[l].astype(BF16)
    g_in, b_in = _row(ln_in_g), _row(ln_in_b)
    g2, b2, g3, b3 = _row(ln2_g[l]), _row(ln2_b[l]), _row(ln3_g[l]), _row(ln3_b[l])

    def rwkv(p, prev, nseq, tlen, s0, tm, tc):
        if prev is None:
            aj, wj, wrj, bj, kj, v, br, kr, g, bv = rwkv_prep_scan(p, prep_consts, nseq, tlen, LANES)
            jmaj = [aj, wj, wrj, bj, kj]
        else:
            a, w, wr, b, k2, v, br, kr, g, bv = rwkv_prep(p, prev, prep_consts, tm)
            jmaj = [_to_scan_j(z, nseq, tlen) for z in (a, w, wr, b, k2)]
        y, st = wkv_scan(*jmaj, _to_scan_t(v, nseq, tlen),
                         _to_scan_s(br, nseq, tlen), _to_scan_s(kr, nseq, tlen), s0, tc)
        return _from_scan_j(y, nseq, tlen), bv, g, _state_from_scan(st, nseq)

    def tail(h1, om, tm, tm_moe):
        h2 = matmul_res_ln(om, h1, w_mo_b, g2, b2, tm, "mem_out_ln2")
        return moe_ln(h2, wr_t, eb, wsg, wsu, wsd, w_e_gate[l], w_e_up[l], w_e_down[l], g3, b3, tm_moe)

    n_p = nb * t
    tm = 512
    hp, pp, qp, kp, vp = ln_proj(x_prompt.reshape(n_p, D_MODEL), g_in, b_in, w_in_b, wkt_b, tm, seq_len=t)
    s0_p = jnp.zeros((nb * N_RWKV_HEADS // SCAN_BH, SCAN_JLO, RWKV_HD, LANES), F32)
    y_p, bv_p, gt_p, wkv_p = rwkv(pp, None, nb, t, s0_p, tm, 128)
    od_p = diff_attn_prompt(qp, kp, vp, rel_bias, lam, nb, t, 512)
    h1_p = mix_ln(y_p, bv_p, gt_p, od_p, hp, mix_consts, tm)
    mem_rows = mem_prompt.reshape(nb * N_MEM, D_MODEL)
    mk_p = matmul(mem_rows, w_mk_b, tm, "mem_k")
    mv_p = matmul(mem_rows, w_mv_b, tm, "mem_v")
    qm_p = matmul(h1_p, w_mq_b, tm, "mem_q").reshape(nb, t, D_MODEL)
    om_p = mem_attn(qm_p, mk_p.reshape(nb, N_MEM, D_MODEL), mv_p.reshape(nb, N_MEM, D_MODEL), tm)
    h3_p = tail(h1_p, om_p.reshape(n_p, D_MODEL), tm, 1024)

    n_s = db * s_len
    hs, ps, qs, ks, vs = ln_proj(x_sample.reshape(n_s, D_MODEL), g_in, b_in, w_in_b, wkt_b, n_s)
    ps3 = ps.reshape(db, s_len, RWKV_COLS)
    prev_s = jnp.concatenate([state_shift[l][:, None, :], ps3[:, :-1]], axis=1).reshape(n_s, RWKV_COLS)
    y_s, bv_s, gt_s, wkv_s = rwkv(ps, prev_s, db, s_len, _state_to_scan(state_wkv[l], db), n_s, s_len)
    n_pool = cache_k.shape[1]
    od_s = diff_attn_sample(qs.reshape(db, s_len, 512), ks.reshape(db, s_len, 512), vs.reshape(db, s_len, 512),
                            jnp.transpose(cache_k[l], (0, 2, 3, 4, 1)).reshape(n_pool, DIFF_WIDTH, PAGE_SIZE),
                            cache_v[l].reshape(n_pool, PAGE_SIZE * N_DIFF_HEADS, DIFF_VD),
                            page_table, rel_bias, lam, 32)
    h1_s = mix_ln(y_s, bv_s, gt_s, od_s.reshape(n_s, DIFF_WIDTH), hs, mix_consts, n_s)
    qm_s = matmul(h1_s, w_mq_b, n_s, "mem_q").reshape(db, s_len, D_MODEL)
    qm_s = jnp.pad(qm_s, ((0, 0), (0, SUBLANES - s_len), (0, 0)))
    om_s = mem_attn_cached(qm_s, _mem_cache_rows(cache_mem_k[l]), _mem_cache_rows(cache_mem_v[l]))
    h3_s = tail(h1_s, om_s[:, :s_len].reshape(n_s, D_MODEL), n_s, n_s)

    return (h3_p.reshape(nb, t, D_MODEL), h3_s.reshape(db, s_len, D_MODEL),
            jnp.transpose(kp.reshape(1, nb, N_DIFF_HEADS, 2, DIFF_HD, t), (0, 1, 5, 2, 3, 4)),
            vp.reshape(1, nb, t, N_DIFF_HEADS, DIFF_VD),
            wkv_p[None], pp.reshape(nb, t, RWKV_COLS)[:, -1][None],
            mk_p.reshape(1, nb, N_MEM, N_MEM_HEADS, MEM_HD), mv_p.reshape(1, nb, N_MEM, N_MEM_HEADS, MEM_HD),
            ks.reshape(1, db, s_len, N_DIFF_HEADS, 2, DIFF_HD), vs.reshape(1, db, s_len, N_DIFF_HEADS, DIFF_VD),
            wkv_s[None], ps3[:, -1][None])
```

```python
import functools
import math

import jax
import jax.numpy as jnp
from jax import lax
from jax.experimental import pallas as pl
from jax.experimental.pallas import tpu as pltpu

F32 = jnp.float32
BF16 = jnp.bfloat16

D_MODEL = 1024
RWKV_WIDTH = 512
RWKV_HD = 64
N_RWKV_HEADS = 8
RWKV_COLS = 1792
GN_EPS = 64e-5
DIFF_WIDTH = 512
DIFF_HD = 64
DIFF_VD = 128
N_DIFF_HEADS = 4
RMS_EPS = 1e-5
N_BUCKETS = 32
MAX_DISTANCE = 128
PAGE_SIZE = 128
N_MEM = 256
N_MEM_HEADS = 4
MEM_HD = 256
N_EXPERTS = 64
N_GROUPS = 8
GROUP_SIZE = 8
TOPK_GROUPS = 4
TOP_K = 8
D_EXPERT = 256
ROUTED_SCALE = 2.5
MOE_EXPERTS_PER_STEP = 4
DEPTH = 1
DN_ALPHA = (2.0 * DEPTH) ** 0.25
LN_EPS = 1e-5
NEG_INF = -1e30
LAM_INIT = 0.8 - 0.6 * math.exp(-0.3 * 0)

LANES = 128
SUBLANES = 8
VMEM_LIMIT = 60 * 1024 * 1024

SCAN_JLO = 16
SCAN_JHI = RWKV_HD // SCAN_JLO
SCAN_BH = LANES // SCAN_JHI


def _cparams(sem):
    return pltpu.CompilerParams(dimension_semantics=sem, vmem_limit_bytes=VMEM_LIMIT)


def _layer_norm(x, g, b):
    mu = jnp.mean(x, -1, keepdims=True)
    d = x - mu
    var = jnp.mean(d * d, -1, keepdims=True)
    return d * lax.rsqrt(var + LN_EPS) * g + b


def _dot(a, b):
    return jnp.dot(a, b, preferred_element_type=F32)


def _dot_nt(a, b):
    return lax.dot_general(a, b, (((1,), (1,)), ((), ())), preferred_element_type=F32)


def _seg_sum(x, ones_blk):
    outs = []
    for c in range(x.shape[1] // LANES):
        xc = x[:, c * LANES:(c + 1) * LANES]
        hi = xc.astype(BF16)
        lo = (xc - hi.astype(F32)).astype(BF16)
        outs.append(_dot(hi, ones_blk) + _dot(lo, ones_blk))
    return outs[0] if len(outs) == 1 else jnp.concatenate(outs, axis=1)


def _ln_proj_body(k_transposed, x_ref, g_ref, b_ref, w_ref, wkt_ref, h_ref, p_ref, q_ref, k_ref, v_ref):
    h = _layer_norm(x_ref[...], g_ref[...], b_ref[...])
    h_ref[...] = h
    hb = h.astype(BF16)
    c0 = RWKV_COLS
    p_ref[...] = _dot(hb, w_ref[:, 0:c0])
    q_ref[...] = _dot(hb, w_ref[:, c0:c0 + 512])
    if k_transposed:
        k_ref[0] = _dot_nt(wkt_ref[...], hb)
    else:
        k_ref[...] = _dot(hb, w_ref[:, c0 + 512:c0 + 1024])
    v_ref[...] = _dot(hb, w_ref[:, c0 + 1024:c0 + 1536])


def ln_proj(x, g, b, w_bf16, wkt_bf16, tm, seq_len=None):
    n = x.shape[0]
    row = lambda width: pl.BlockSpec((tm, width), lambda i: (i, 0))
    full = lambda a: pl.BlockSpec(a.shape, lambda i: (0,) * a.ndim)
    if seq_len is None:
        k_spec, k_shape = row(512), (n, 512)
    else:
        bps = seq_len // tm
        k_spec = pl.BlockSpec((1, 512, tm), lambda i: (i // bps, 0, i % bps))
        k_shape = (n // seq_len, 512, seq_len)
    shapes = [(n, D_MODEL), (n, RWKV_COLS), (n, 512), k_shape, (n, 512)]
    return pl.pallas_call(
        functools.partial(_ln_proj_body, seq_len is not None),
        grid=(n // tm,),
        in_specs=[row(D_MODEL), full(g), full(b), full(w_bf16), full(wkt_bf16)],
        out_specs=[row(D_MODEL), row(RWKV_COLS), row(512), k_spec, row(512)],
        out_shape=[jax.ShapeDtypeStruct(sh, F32) for sh in shapes],
        compiler_params=_cparams(("arbitrary",)),
        name="ln_proj",
    )(x, g, b, w_bf16, wkt_bf16)


def _mm_body(x_ref, w_ref, o_ref):
    o_ref[...] = _dot(x_ref[...].astype(BF16), w_ref[...])


def matmul(x, w_bf16, tm, name):
    n, kdim = x.shape
    nout = w_bf16.shape[1]
    return pl.pallas_call(
        _mm_body,
        grid=(n // tm,),
        in_specs=[pl.BlockSpec((tm, kdim), lambda i: (i, 0)), pl.BlockSpec((kdim, nout), lambda i: (0, 0))],
        out_specs=pl.BlockSpec((tm, nout), lambda i: (i, 0)),
        out_shape=jax.ShapeDtypeStruct((n, nout), F32),
        compiler_params=_cparams(("arbitrary",)),
        name=name,
    )(x, w_bf16)


def _mm_res_ln_body(x_ref, res_ref, w_ref, g_ref, b_ref, o_ref):
    y = _dot(x_ref[...].astype(BF16), w_ref[...])
    o_ref[...] = _layer_norm(DN_ALPHA * res_ref[...] + y, g_ref[...], b_ref[...])


def matmul_res_ln(x, res, w_bf16, g, b, tm, name):
    n, kdim = x.shape
    nout = w_bf16.shape[1]
    row = lambda width: pl.BlockSpec((tm, width), lambda i: (i, 0))
    full = lambda a: pl.BlockSpec(a.shape, lambda i: (0,) * a.ndim)
    return pl.pallas_call(
        _mm_res_ln_body,
        grid=(n // tm,),
        in_specs=[row(kdim), row(nout), full(w_bf16), full(g), full(b)],
        out_specs=row(nout),
        out_shape=jax.ShapeDtypeStruct((n, nout), F32),
        compiler_params=_cparams(("arbitrary",)),
        name=name,
    )(x, res, w_bf16, g, b)


def _rwkv_prep_math(p, p_prev, mu_ref, wlo_ref, w0a0_ref, wg_ref, kk_ref, ka_ref, rk_ref, seg_ref):
    ps = p + (p_prev - p) * mu_ref[...]
    r = ps[:, 0:512]
    k = ps[:, 512:1024]
    v = ps[:, 1024:1536]
    lo = ps[:, 1536:1664]
    g_lo = ps[:, 1664:1792]
    lane = lax.broadcasted_iota(jnp.int32, lo.shape, 1)
    lo = jnp.where(lane < 64, jnp.tanh(lo), lo)
    wa = _dot(lo.astype(BF16), wlo_ref[...]) + w0a0_ref[...]
    x = -wa[:, 0:512]
    softplus = jnp.maximum(x, 0.0) + jnp.log(1.0 + jnp.exp(-jnp.abs(x)))
    decay = jnp.exp(-jnp.exp(-softplus - 0.5))
    a = jax.nn.sigmoid(wa[:, 512:1024])
    g = _dot(jax.nn.sigmoid(g_lo).astype(BF16), wg_ref[...])
    seg = seg_ref[...]
    kk = k * kk_ref[...]
    kk = kk / jnp.maximum(jnp.sqrt(_seg_sum(kk * kk, seg)), 1e-12)
    k2 = k * (1.0 + (a - 1.0) * ka_ref[...])
    b = kk * a
    return (-kk, decay, decay * r, b, k2, v, _seg_sum(b * r, seg), _seg_sum(k2 * r, seg), g,
            _seg_sum(r * k2 * rk_ref[...], seg) * v)


def _rwkv_prep_body(p_ref, prev_ref, *rest):
    consts, outs = rest[:8], rest[8:]
    for o_ref, val in zip(outs, _rwkv_prep_math(p_ref[...], prev_ref[...], *consts)):
        o_ref[...] = val


def rwkv_prep(p, prev, consts, tm):
    n = p.shape[0]
    row = lambda width: pl.BlockSpec((tm, width), lambda i: (i, 0))
    full = lambda a: pl.BlockSpec(a.shape, lambda i: (0,) * a.ndim)
    return pl.pallas_call(
        _rwkv_prep_body,
        grid=(n // tm,),
        in_specs=[row(RWKV_COLS), row(RWKV_COLS)] + [full(c) for c in consts],
        out_specs=[row(512)] * 10,
        out_shape=[jax.ShapeDtypeStruct((n, 512), F32)] * 10,
        compiler_params=_cparams(("arbitrary",)),
        name="rwkv_prep",
    )(p, prev, *consts)


RELAYOUT_PITCH = RWKV_HD + SUBLANES


def _to_scan_layout(x, nb, tt, o_ref, scr, t_major=False):
    for b in range(nb):
        xt = jnp.transpose(x[b * tt:(b + 1) * tt, :])
        for h in range(N_RWKV_HEADS):
            scr[b, h * RELAYOUT_PITCH:h * RELAYOUT_PITCH + RWKV_HD, :] = xt[h * RWKV_HD:(h + 1) * RWKV_HD, :]
    for j in range(SCAN_JLO):
        rows = [scr[b, pl.ds(jq * SCAN_JLO + j, N_RWKV_HEADS, stride=RELAYOUT_PITCH), :]
                for jq in range(SCAN_JHI) for b in range(nb)]
        tile = jnp.transpose(jnp.concatenate(rows, axis=0))
        if t_major:
            o_ref[0, :, j, :] = tile
        else:
            o_ref[0, j] = tile


def _rwkv_prep_scan_body(nb, tt, p_ref, prev_ref, *rest):
    consts, outs, scr = rest[:8], rest[8:18], rest[18]
    p = p_ref[...].reshape(nb * tt, RWKV_COLS)
    row = lax.broadcasted_iota(jnp.int32, p.shape, 0)
    p_prev = pltpu.roll(p, 1, 0)
    for b in range(nb):
        first = jnp.where(pl.program_id(0) == 0, 0.0, prev_ref[b, SUBLANES - 1:SUBLANES, :])
        p_prev = jnp.where(row == b * tt, first, p_prev)
    vals = _rwkv_prep_math(p, p_prev, *consts)
    for o_ref, val in zip(outs[:5], vals[:5]):
        _to_scan_layout(val, nb, tt, o_ref, scr)
    _to_scan_layout(vals[5], nb, tt, outs[5], scr, t_major=True)
    for o_ref, val in zip(outs[6:], vals[6:]):
        o_ref[...] = val.reshape(nb, tt, RWKV_WIDTH)


def rwkv_prep_scan(p, consts, nb, t, tt):
    assert nb * N_RWKV_HEADS == SCAN_BH and t % tt == 0 and tt % LANES == 0
    p3 = p.reshape(nb, t, RWKV_COLS)
    per8 = tt // SUBLANES
    full = lambda a: pl.BlockSpec(a.shape, lambda i: (0,) * a.ndim)
    nat = pl.BlockSpec((nb, tt, RWKV_WIDTH), lambda i: (0, i, 0))
    scn = pl.BlockSpec((1, SCAN_JLO, tt, LANES), lambda i: (0, 0, i, 0))
    outs = pl.pallas_call(
        functools.partial(_rwkv_prep_scan_body, nb, tt),
        grid=(t // tt,),
        in_specs=[pl.BlockSpec((nb, tt, RWKV_COLS), lambda i: (0, i, 0)),
                  pl.BlockSpec((nb, SUBLANES, RWKV_COLS), lambda i: (0, jnp.maximum(i * per8 - 1, 0), 0))]
                 + [full(c) for c in consts],
        out_specs=[scn] * 5 + [pl.BlockSpec((1, tt, SCAN_JLO, LANES), lambda i: (0, i, 0, 0))] + [nat] * 4,
        out_shape=[jax.ShapeDtypeStruct((1, SCAN_JLO, t, LANES), F32)] * 5
                  + [jax.ShapeDtypeStruct((1, t, SCAN_JLO, LANES), F32)]
                  + [jax.ShapeDtypeStruct((nb, t, RWKV_WIDTH), F32)] * 4,
        scratch_shapes=[pltpu.VMEM((nb, N_RWKV_HEADS * RELAYOUT_PITCH, tt), F32)],
        compiler_params=_cparams(("arbitrary",)),
        name="rwkv_prep_scan",
    )(p3, p3, *consts)
    return list(outs[:6]) + [o.reshape(nb * t, RWKV_WIDTH) for o in outs[6:]]


def _lane_fold(x):
    return x + pltpu.roll(x, 32, 1) + pltpu.roll(x, 64, 1) + pltpu.roll(x, 96, 1)


def _wkv_scan_body(tc, a_ref, w_ref, wr_ref, b_ref, k_ref, v_ref, br_ref, kr_ref, s0_ref,
                   y_ref, st_ref, s_scr):
    c = pl.program_id(1)
    ngrp = RWKV_HD // SUBLANES

    @pl.when(c == 0)
    def _():
        s_scr[...] = s0_ref[0]

    def bcast(ref, t, j):
        return jnp.broadcast_to(ref[0, j, pl.ds(t, 1), :], (SUBLANES, LANES))

    lane_grp = lax.broadcasted_iota(jnp.int32, (SUBLANES, LANES), 1) // SCAN_BH

    def step(t, carry):
        acc_a = [jnp.zeros((SUBLANES, LANES), F32) for _ in range(ngrp)]
        acc_y = [jnp.zeros((SUBLANES, LANES), F32) for _ in range(ngrp)]
        for j in range(SCAN_JLO):
            ab = bcast(a_ref, t, j)
            wrb = bcast(wr_ref, t, j)
            for ig in range(ngrp):
                s = s_scr[j, ig * SUBLANES:(ig + 1) * SUBLANES, :]
                acc_a[ig] = acc_a[ig] + s * ab
                acc_y[ig] = acc_y[ig] + s * wrb
        sa = [_lane_fold(x) for x in acc_a]
        vp = [v_ref[0, t, m * SUBLANES:(m + 1) * SUBLANES, :] for m in range(2)]
        vrot = [[x] + [pltpu.roll(x, SCAN_BH * k, 1) for k in range(1, SCAN_JHI)] for x in vp]
        vv = []
        for ig in range(ngrp):
            q, rot = ig // 2, vrot[ig % 2]
            x = rot[(SCAN_JHI - 1 - q) % SCAN_JHI]
            for g in range(SCAN_JHI - 2, -1, -1):
                x = jnp.where(lane_grp == g, rot[(g - q) % SCAN_JHI], x)
            vv.append(x)
        for j in range(SCAN_JLO):
            wb = bcast(w_ref, t, j)
            bb = bcast(b_ref, t, j)
            kb = bcast(k_ref, t, j)
            for ig in range(ngrp):
                sl = slice(ig * SUBLANES, (ig + 1) * SUBLANES)
                s_scr[j, sl, :] = s_scr[j, sl, :] * wb + sa[ig] * bb + vv[ig] * kb
        br = jnp.broadcast_to(br_ref[0, t], (SUBLANES, LANES))
        kr = jnp.broadcast_to(kr_ref[0, t], (SUBLANES, LANES))
        y = [_lane_fold(acc_y[ig]) + sa[ig] * br + vv[ig] * kr for ig in range(ngrp)]
        for m in range(2):
            yp = y[6 + m]
            for q in (2, 1, 0):
                yp = jnp.where(lane_grp == q, y[2 * q + m], yp)
            y_ref[0, t, m * SUBLANES:(m + 1) * SUBLANES, :] = yp
        return carry

    lax.fori_loop(0, tc, step, 0)

    @pl.when(c == pl.num_programs(1) - 1)
    def _():
        st_ref[0] = s_scr[...]


def wkv_scan(a, w, wr, b, k, v, br, kr, s0, tc):
    g, t = v.shape[0], v.shape[1]
    jspec = pl.BlockSpec((1, SCAN_JLO, tc, LANES), lambda gi, c: (gi, 0, c, 0))
    tspec = pl.BlockSpec((1, tc, SCAN_JLO, LANES), lambda gi, c: (gi, c, 0, 0))
    sspec = pl.BlockSpec((1, tc, 1, LANES), lambda gi, c: (gi, c, 0, 0))
    stspec = pl.BlockSpec((1, SCAN_JLO, RWKV_HD, LANES), lambda gi, c: (gi, 0, 0, 0))
    return pl.pallas_call(
        functools.partial(_wkv_scan_body, tc),
        grid=(g, t // tc),
        in_specs=[jspec] * 5 + [tspec, sspec, sspec, stspec],
        out_specs=[tspec, stspec],
        out_shape=[jax.ShapeDtypeStruct((g, t, SCAN_JLO, LANES), F32),
                   jax.ShapeDtypeStruct((g, SCAN_JLO, RWKV_HD, LANES), F32)],
        scratch_shapes=[pltpu.VMEM((SCAN_JLO, RWKV_HD, LANES), F32)],
        compiler_params=_cparams(("arbitrary", "arbitrary")),
        name="wkv_scan",
    )(a, w, wr, b, k, v, br, kr, s0)


def _to_scan_t(x, nb, t):
    g = nb * N_RWKV_HEADS // SCAN_BH
    bp = nb // g
    x = x.reshape(g, bp, t, N_RWKV_HEADS, SCAN_JHI, SCAN_JLO)
    return jnp.transpose(x, (0, 2, 5, 4, 1, 3)).reshape(g, t, SCAN_JLO, LANES)


def _to_scan_j(x, nb, t):
    g = nb * N_RWKV_HEADS // SCAN_BH
    bp = nb // g
    x = x.reshape(g, bp, t, N_RWKV_HEADS, SCAN_JHI, SCAN_JLO)
    return jnp.transpose(x, (0, 5, 2, 4, 1, 3)).reshape(g, SCAN_JLO, t, LANES)


def _from_scan_j(x, nb, t):
    g = x.shape[0]
    bp = nb // g
    x = x.reshape(g, t, SCAN_JLO, SCAN_JHI, bp, N_RWKV_HEADS)
    return jnp.transpose(x, (0, 4, 1, 5, 3, 2)).reshape(nb * t, RWKV_WIDTH)


def _to_scan_s(x, nb, t):
    g = nb * N_RWKV_HEADS // SCAN_BH
    bp = nb // g
    x = x.reshape(g, bp, t, N_RWKV_HEADS, RWKV_HD)[..., 0]
    x = jnp.transpose(x, (0, 2, 1, 3)).reshape(g, t, 1, 1, SCAN_BH)
    return jnp.broadcast_to(x, (g, t, 1, SCAN_JHI, SCAN_BH)).reshape(g, t, 1, LANES)


def _state_to_scan(s, nb):
    g = nb * N_RWKV_HEADS // SCAN_BH
    bp = nb // g
    s = s.reshape(g, bp, N_RWKV_HEADS, RWKV_HD, SCAN_JHI, SCAN_JLO)
    return jnp.transpose(s, (0, 5, 3, 4, 1, 2)).reshape(g, SCAN_JLO, RWKV_HD, LANES)


def _state_from_scan(s, nb):
    g = s.shape[0]
    bp = nb // g
    s = s.reshape(g, SCAN_JLO, RWKV_HD, SCAN_JHI, bp, N_RWKV_HEADS)
    return jnp.transpose(s, (0, 4, 5, 2, 3, 1)).reshape(nb, N_RWKV_HEADS, RWKV_HD, RWKV_HD)


def _mix_body(y_ref, bv_ref, g_ref, od_ref, h_ref, lnxg_ref, lnxb_ref, sub_ref, wout_ref, g1_ref, b1_ref,
              seg_ref, ones_ref, wmq_ref, o_ref, qm_ref):
    seg = seg_ref[...]
    y = y_ref[...]
    mu = _seg_sum(y, seg) * (1.0 / RWKV_HD)
    d = y - mu
    var = _seg_sum(d * d, seg) * (1.0 / RWKV_HD)
    yn = d * lax.rsqrt(var + GN_EPS) * lnxg_ref[...] + lnxb_ref[...]
    o_rw = (yn + bv_ref[...]) * g_ref[...]
    od = od_ref[...]
    ms = _seg_sum(od * od, ones_ref[...]) * (1.0 / DIFF_VD)
    o_df = od * lax.rsqrt(ms + RMS_EPS) * sub_ref[...]
    mix = _dot(o_rw.astype(BF16), wout_ref[0:512, :]) + _dot(o_df.astype(BF16), wout_ref[512:1024, :])
    h1 = _layer_norm(DN_ALPHA * h_ref[...] + mix, g1_ref[...], b1_ref[...])
    o_ref[...] = h1
    qm_ref[...] = _dot(h1.astype(BF16), wmq_ref[...])


def mix_ln(y, bv, g, od, h, consts, tm):
    n = y.shape[0]
    row = lambda width: pl.BlockSpec((tm, width), lambda i: (i, 0))
    full = lambda a: pl.BlockSpec(a.shape, lambda i: (0,) * a.ndim)
    return pl.pallas_call(
        _mix_body,
        grid=(n // tm,),
        in_specs=[row(512)] * 4 + [row(D_MODEL)] + [full(c) for c in consts],
        out_specs=[row(D_MODEL), row(D_MODEL)],
        out_shape=[jax.ShapeDtypeStruct((n, D_MODEL), F32)] * 2,
        compiler_params=_cparams(("arbitrary",)),
        name="mix_ln1",
    )(y, bv, g, od, h, *consts)


def _t5_bucket(dist):
    max_exact = N_BUCKETS // 2
    d = jnp.maximum(dist, 1).astype(F32)
    large = max_exact + (jnp.log(d / max_exact) / math.log(MAX_DISTANCE / max_exact)
                         * (N_BUCKETS - max_exact)).astype(jnp.int32)
    return jnp.where(dist < max_exact, dist, jnp.minimum(large, N_BUCKETS - 1))


def _bias_table(rel_bias, dist):
    bucket = _t5_bucket(jnp.maximum(dist, 0))
    tail = (1,) * dist.ndim
    bias = jnp.zeros(rel_bias.shape[1:] + dist.shape, F32)
    for bk in range(N_BUCKETS):
        bias = jnp.where(bucket == bk, rel_bias[bk].astype(F32).reshape(rel_bias.shape[1:] + tail), bias)
    return jnp.where(dist >= 0, bias, NEG_INF)


def _bias_tiles_body(tq, tk, thr_ref, val_ref, o_ref):
    h = pl.program_id(0)
    rr = lax.broadcasted_iota(jnp.int32, (tq, tk), 0)
    cc = lax.broadcasted_iota(jnp.int32, (tq, tk), 1)
    for kind, base in enumerate((tq, 0)):
        d = base + rr - cc
        for m in range(2):
            bias = jnp.zeros((tq, tk), F32) + val_ref[h * 2 + m]
            for bk in range(1, N_BUCKETS):
                bias = jnp.where(d >= thr_ref[bk], val_ref[(bk * N_DIFF_HEADS + h) * 2 + m], bias)
            o_ref[0, kind, m * tq:(m + 1) * tq, :] = jnp.where(d >= 0, bias, NEG_INF)


def bias_tiles(rel_bias, tq, tk):
    dd = jnp.arange(MAX_DISTANCE + 1, dtype=jnp.int32)
    bucket = _t5_bucket(dd)
    thr = jnp.sum(bucket[None, :] < jnp.arange(N_BUCKETS, dtype=jnp.int32)[:, None], axis=1).astype(jnp.int32)
    smem = pl.BlockSpec(memory_space=pltpu.SMEM)
    return pl.pallas_call(
        functools.partial(_bias_tiles_body, tq, tk),
        grid=(N_DIFF_HEADS,),
        in_specs=[smem, smem],
        out_specs=pl.BlockSpec((1, 2, 2 * tq, tk), lambda h: (h, 0, 0, 0)),
        out_shape=jax.ShapeDtypeStruct((N_DIFF_HEADS, 2, 2 * tq, tk), F32),
        compiler_params=_cparams(("arbitrary",)),
        name="bias_tiles",
    )(thr, rel_bias.astype(F32).reshape(-1))


def _dap_body(tq, tk, qi_ref, ki_ref, lam_ref, q_ref, k_ref, v_ref, bias_ref, cfar_ref, o_ref,
              qs_scr, m_scr, l_scr, acc_scr):
    step = pl.program_id(2)
    qi = qi_ref[step]
    ki = ki_ref[step]

    @pl.when(ki == 0)
    def _():
        q = q_ref[...] * (DIFF_HD ** -0.5)
        lane = lax.broadcasted_iota(jnp.int32, q.shape, 1)
        qs_scr[0:tq, :] = jnp.where(lane < DIFF_HD, q, 0.0).astype(BF16)
        qs_scr[tq:2 * tq, :] = jnp.where(lane >= DIFF_HD, q, 0.0).astype(BF16)
        m_scr[...] = jnp.full(m_scr.shape, NEG_INF, F32)
        l_scr[...] = jnp.zeros(l_scr.shape, F32)
        acc_scr[...] = jnp.zeros(acc_scr.shape, F32)

    def update(s, shift):
        m_prev = m_scr[...]
        m_new = jnp.maximum(m_prev, jnp.max(s, axis=1, keepdims=True) + shift)
        alpha = jnp.exp(m_prev - m_new)
        p = jnp.exp(s - jnp.tile(m_new - shift, (1, tk // LANES)))
        l_scr[...] = alpha * l_scr[...] + jnp.sum(p, axis=1, keepdims=True)
        acc_scr[...] = alpha * acc_scr[...] + _dot(p.astype(BF16), v_ref[...].astype(BF16))
        m_scr[...] = m_new

    @pl.when(ki < qi - 1)
    def _():
        update(_dot(qs_scr[...], k_ref[0].astype(BF16)), cfar_ref[0])

    @pl.when(ki >= qi - 1)
    def _():
        update(_dot(qs_scr[...], k_ref[0].astype(BF16)) + bias_ref[0, 0], jnp.zeros((2 * tq, LANES), F32))

    @pl.when(ki == qi)
    def _():
        o = acc_scr[...] / l_scr[...]
        o_ref[...] = o[0:tq] - lam_ref[0, 0] * o[tq:2 * tq]


def diff_attn_prompt(q, kt, v, rel_bias, lam, nb, t, tq):
    tk = tq
    assert tq >= MAX_DISTANCE and t % tq == 0
    nq = t // tq
    qi_list, ki_list = [], []
    for a in range(nq):
        for c in range(a + 1):
            qi_list.append(a)
            ki_list.append(c)
    qi_arr = jnp.asarray(qi_list, jnp.int32)
    ki_arr = jnp.asarray(ki_list, jnp.int32)
    tiles = bias_tiles(rel_bias, tq, tk)
    cfar = jnp.broadcast_to(rel_bias[N_BUCKETS - 1].astype(F32)[:, :, None, None], (N_DIFF_HEADS, 2, tq, LANES))
    cfar = cfar.reshape(N_DIFF_HEADS, 2 * tq, LANES)

    grid_spec = pltpu.PrefetchScalarGridSpec(
        num_scalar_prefetch=2,
        grid=(nb, N_DIFF_HEADS, len(qi_list)),
        in_specs=[
            pl.BlockSpec(memory_space=pltpu.SMEM),
            pl.BlockSpec((tq, LANES), lambda b, h, s, qi, ki: (b * nq + qi[s], h)),
            pl.BlockSpec((1, LANES, tk), lambda b, h, s, qi, ki: (b, h, ki[s])),
            pl.BlockSpec((tk, LANES), lambda b, h, s, qi, ki: (b * nq + ki[s], h)),
            pl.BlockSpec((1, 1, 2 * tq, tk), lambda b, h, s, qi, ki: (h, jnp.where(ki[s] == qi[s], 1, 0), 0, 0)),
            pl.BlockSpec((1, 2 * tq, LANES), lambda b, h, s, qi, ki: (h, 0, 0)),
        ],
        out_specs=pl.BlockSpec((tq, LANES), lambda b, h, s, qi, ki: (b * nq + qi[s], h)),
        scratch_shapes=[pltpu.VMEM((2 * tq, LANES), BF16), pltpu.VMEM((2 * tq, LANES), F32),
                        pltpu.VMEM((2 * tq, LANES), F32), pltpu.VMEM((2 * tq, LANES), F32)],
    )
    return pl.pallas_call(
        functools.partial(_dap_body, tq, tk),
        grid_spec=grid_spec,
        out_shape=jax.ShapeDtypeStruct((nb * t, DIFF_WIDTH), F32),
        compiler_params=_cparams(("arbitrary", "arbitrary", "arbitrary")),
        name="diff_attn_prompt",
    )(qi_arr, ki_arr, lam, q, kt, v, tiles, cfar)


def _das_body(npg, s_len, pt_ref, lam_ref, q_ref, qmask_ref, bias_ref, bnew_ref, kn_ref, vn_ref, *rest):
    k_refs = rest[0:npg]
    v_refs = rest[npg:2 * npg]
    o_ref = rest[2 * npg]
    qs_scr, m_scr, l_scr, acc_scr = rest[2 * npg + 1:]
    g = pl.program_id(1)
    rows = 2 * N_DIFF_HEADS * s_len

    @pl.when(g == 0)
    def _():
        q = q_ref[0] * (DIFF_HD ** -0.5)
        qrep = jnp.concatenate([q] * (2 * N_DIFF_HEADS), axis=0)
        qs_scr[...] = (qrep * qmask_ref[...]).astype(BF16)
        m_scr[...] = jnp.full(m_scr.shape, NEG_INF, F32)
        l_scr[...] = jnp.zeros(l_scr.shape, F32)
        acc_scr[...] = jnp.zeros(acc_scr.shape, F32)

    hrows = 2 * s_len
    qs = qs_scr[...]
    s = jnp.concatenate([_dot(qs, kr[0].astype(BF16)) for kr in k_refs], axis=1) + bias_ref[0]
    m_prev = m_scr[...]
    m_new = jnp.maximum(m_prev, jnp.max(s, axis=1, keepdims=True))
    alpha = jnp.exp(m_prev - m_new)
    p = jnp.exp(s - jnp.tile(m_new, (1, npg)))
    l_scr[...] = alpha * l_scr[...] + jnp.sum(p, axis=1, keepdims=True)
    pvs = []
    for h in range(N_DIFF_HEADS):
        pv = None
        for j in range(npg):
            ph = p[h * hrows:(h + 1) * hrows, j * PAGE_SIZE:(j + 1) * PAGE_SIZE].astype(BF16)
            vh = v_refs[j][0, pl.ds(h, PAGE_SIZE, stride=N_DIFF_HEADS), :].astype(BF16)
            d = _dot(ph, vh)
            pv = d if pv is None else pv + d
        pvs.append(pv)
    acc_scr[...] = alpha * acc_scr[...] + jnp.concatenate(pvs, axis=0)
    m_scr[...] = m_new

    @pl.when(g == pl.num_programs(1) - 1)
    def _():
        qf = qs_scr[...].astype(F32)
        kn = kn_ref[0].astype(BF16).astype(F32)
        vn = vn_ref[0].astype(BF16).astype(F32)
        lane = lax.broadcasted_iota(jnp.int32, (rows, LANES), 1)
        sn = bnew_ref[...]
        for j in range(s_len):
            dj = jnp.sum(qf * kn[j:j + 1, :], axis=1, keepdims=True)
            sn = sn + jnp.where(lane == j, dj, 0.0)
        m_prev = m_scr[...]
        m_new = jnp.maximum(m_prev, jnp.max(sn, axis=1, keepdims=True))
        alpha = jnp.exp(m_prev - m_new)
        pn = jnp.exp(sn - m_new).astype(BF16).astype(F32)
        l_fin = alpha * l_scr[...] + jnp.sum(pn, axis=1, keepdims=True)
        acc = alpha * acc_scr[...]
        for j in range(s_len):
            vj = jnp.concatenate([jnp.broadcast_to(vn[j:j + 1, h * DIFF_VD:(h + 1) * DIFF_VD], (hrows, DIFF_VD))
                                  for h in range(N_DIFF_HEADS)], axis=0)
            acc = acc + pn[:, j:j + 1] * vj
        o = acc / l_fin
        lam = lam_ref[0, 0]
        outs = []
        for h in range(N_DIFF_HEADS):
            blk = o[h * hrows:(h + 1) * hrows, :]
            outs.append(blk[0:s_len] - lam * blk[s_len:hrows])
        o_ref[0] = jnp.concatenate(outs, axis=1)


def diff_attn_sample(q, k, v, cache_k, cache_v, page_table, rel_bias, lam, npg):
    db, s_len, _ = q.shape
    n_pages = page_table.shape[1]
    past = n_pages * PAGE_SIZE
    assert n_pages % npg == 0
    ngroups = n_pages // npg
    rows = 2 * N_DIFF_HEADS * s_len
    width = npg * PAGE_SIZE
    hm = jnp.arange(rows, dtype=jnp.int32) // s_len
    qmask = (jnp.arange(DIFF_WIDTH, dtype=jnp.int32)[None, :] // DIFF_HD == hm[:, None]).astype(F32)
    qpos = past + jnp.arange(s_len, dtype=jnp.int32)

    def rows_of(table):
        return table.reshape(rows, table.shape[-1])

    far = rows_of(_bias_table(rel_bias, jnp.full((s_len, width), 2 * MAX_DISTANCE, jnp.int32)))
    kpos_last = past - width + jnp.arange(width, dtype=jnp.int32)
    last = rows_of(_bias_table(rel_bias, qpos[:, None] - kpos_last[None, :]))
    bias = jnp.stack([far, last], axis=0)
    knew_pos = past + jnp.arange(LANES, dtype=jnp.int32)
    dist_new = jnp.where(jnp.arange(LANES)[None, :] < s_len, qpos[:, None] - knew_pos[None, :], -1)
    bnew = rows_of(_bias_table(rel_bias, dist_new))
    assert width >= MAX_DISTANCE + s_len
    pt_flat = page_table.reshape(-1).astype(jnp.int32)

    def page_spec(j):
        return pl.BlockSpec((1, DIFF_WIDTH, PAGE_SIZE),
                            lambda b, g, pt, j=j: (pt[b * n_pages + g * npg + j], 0, 0))

    seq_spec = pl.BlockSpec((1, s_len, DIFF_WIDTH), lambda b, g, pt: (b, 0, 0))
    full2 = lambda a: pl.BlockSpec(a.shape, lambda b, g, pt: (0,) * a.ndim)
    grid_spec = pltpu.PrefetchScalarGridSpec(
        num_scalar_prefetch=1,
        grid=(db, ngroups),
        in_specs=[pl.BlockSpec(memory_space=pltpu.SMEM), seq_spec, full2(qmask),
                  pl.BlockSpec((1, rows, width), lambda b, g, pt: (jnp.where(g == ngroups - 1, 1, 0), 0, 0)),
                  full2(bnew), seq_spec, seq_spec]
                 + [page_spec(j) for j in range(npg)] + [page_spec(j) for j in range(npg)],
        out_specs=seq_spec,
        scratch_shapes=[pltpu.VMEM((rows, DIFF_WIDTH), BF16), pltpu.VMEM((rows, LANES), F32),
                        pltpu.VMEM((rows, LANES), F32), pltpu.VMEM((rows, DIFF_VD), F32)],
    )
    return pl.pallas_call(
        functools.partial(_das_body, npg, s_len),
        grid_spec=grid_spec,
        out_shape=jax.ShapeDtypeStruct((db, s_len, DIFF_WIDTH), F32),
        compiler_params=_cparams(("arbitrary", "arbitrary")),
        name="diff_attn_sample",
    )(pt_flat, lam, q, qmask, bias, bnew, k, v, *([cache_k] * npg), *([cache_v] * npg))


def _mem_attn_body(q_ref, mk_ref, mv_ref, o_ref):
    for h in range(N_MEM_HEADS):
        cols = slice(h * MEM_HD, (h + 1) * MEM_HD)
        qh = (q_ref[0, :, cols] * (MEM_HD ** -0.5)).astype(BF16)
        s = _dot_nt(qh, mk_ref[0, :, cols].astype(BF16))
        p = jnp.exp(s - jnp.max(s, axis=1, keepdims=True))
        l = jnp.sum(p, axis=1, keepdims=True)
        o_ref[0, :, cols] = _dot(p.astype(BF16), mv_ref[0, :, cols].astype(BF16)) / l


def mem_attn(q, mk, mv, tm):
    nb, t, _ = q.shape
    qspec = pl.BlockSpec((1, tm, D_MODEL), lambda b, i: (b, i, 0))
    mspec = pl.BlockSpec((1, N_MEM, D_MODEL), lambda b, i: (b, 0, 0))
    return pl.pallas_call(
        _mem_attn_body,
        grid=(nb, t // tm),
        in_specs=[qspec, mspec, mspec],
        out_specs=qspec,
        out_shape=jax.ShapeDtypeStruct((nb, t, D_MODEL), F32),
        compiler_params=_cparams(("arbitrary", "arbitrary")),
        name="mem_attn",
    )(q, mk, mv)


MEM_COL_TILES = MEM_HD // LANES


def _mem_cache_rows(cache):
    nb = cache.shape[0]
    c = cache.reshape(nb, N_MEM, N_MEM_HEADS, MEM_COL_TILES, LANES)
    return jnp.transpose(c, (0, 1, 3, 2, 4)).reshape(nb, N_MEM * MEM_COL_TILES * N_MEM_HEADS, LANES)


def _mem_attn_cached_body(q_ref, mk_ref, mv_ref, o_ref):
    stride = MEM_COL_TILES * N_MEM_HEADS

    def tile(ref, h, ct):
        return ref[0, pl.ds(ct * N_MEM_HEADS + h, N_MEM, stride=stride), :].astype(BF16)

    for h in range(N_MEM_HEADS):
        s = None
        for ct in range(MEM_COL_TILES):
            c0 = h * MEM_HD + ct * LANES
            qh = (q_ref[0, :, c0:c0 + LANES] * (MEM_HD ** -0.5)).astype(BF16)
            d = _dot_nt(qh, tile(mk_ref, h, ct))
            s = d if s is None else s + d
        p = jnp.exp(s - jnp.max(s, axis=1, keepdims=True))
        l = jnp.sum(p, axis=1, keepdims=True)
        pb = p.astype(BF16)
        for ct in range(MEM_COL_TILES):
            c0 = h * MEM_HD + ct * LANES
            o_ref[0, :, c0:c0 + LANES] = _dot(pb, tile(mv_ref, h, ct)) / l


def mem_attn_cached(q, mk_rows, mv_rows):
    nb, t, _ = q.shape
    qspec = pl.BlockSpec((1, t, D_MODEL), lambda b: (b, 0, 0))
    mspec = pl.BlockSpec((1,) + mk_rows.shape[1:], lambda b: (b, 0, 0))
    return pl.pallas_call(
        _mem_attn_cached_body,
        grid=(nb,),
        in_specs=[qspec, mspec, mspec],
        out_specs=qspec,
        out_shape=jax.ShapeDtypeStruct((nb, t, D_MODEL), F32),
        compiler_params=_cparams(("arbitrary",)),
        name="mem_attn_cached",
    )(q, mk_rows, mv_rows)


def _route(x, wr_t, e_bias):
    tm = x.shape[0]
    logits = lax.dot_general(wr_t, x, (((1,), (1,)), ((), ())), precision=lax.Precision.HIGHEST,
                             preferred_element_type=F32)
    scores = jax.nn.sigmoid(logits)
    biased = scores + e_bias
    sub = lax.broadcasted_iota(jnp.int32, (GROUP_SIZE, tm), 0)
    grp_rows = []
    for gi in range(N_GROUPS):
        xg = biased[gi * GROUP_SIZE:(gi + 1) * GROUP_SIZE, :]
        m1 = jnp.max(xg, axis=0, keepdims=True)
        i1 = jnp.min(jnp.where(xg == m1, sub, GROUP_SIZE), axis=0, keepdims=True)
        m2 = jnp.max(jnp.where(sub == i1, -jnp.inf, xg), axis=0, keepdims=True)
        grp_rows.append(m1 + m2)
    grp = jnp.concatenate(grp_rows, axis=0)
    gidx = lax.broadcasted_iota(jnp.int32, (N_GROUPS, tm), 0)
    grank = jnp.zeros((N_GROUPS, tm), jnp.int32)
    for gi in range(N_GROUPS):
        other = jnp.broadcast_to(grp[gi:gi + 1, :], (N_GROUPS, tm))
        grank = grank + ((other > grp) | ((other == grp) & (gi < gidx))).astype(jnp.int32)
    gsel = grank < TOPK_GROUPS
    emask = jnp.concatenate([jnp.broadcast_to(gsel[gi:gi + 1, :], (GROUP_SIZE, tm)) for gi in range(N_GROUPS)],
                            axis=0)
    masked = jnp.where(emask, biased, NEG_INF)
    eidx = lax.broadcasted_iota(jnp.int32, (N_EXPERTS, tm), 0)
    erank = jnp.zeros((N_EXPERTS, tm), jnp.int32)
    for e in range(N_EXPERTS):
        other = jnp.broadcast_to(masked[e:e + 1, :], (N_EXPERTS, tm))
        erank = erank + ((other > masked) | ((other == masked) & (e < eidx))).astype(jnp.int32)
    sel = jnp.where(erank < TOP_K, scores, 0.0)
    gates = sel / jnp.sum(sel, axis=0, keepdims=True) * ROUTED_SCALE
    return jnp.concatenate([gates, jnp.zeros_like(gates)], axis=0)


def _moe_body(x_ref, wrt_ref, eb_ref, wsg_ref, wsu_ref, wsd_ref, wg_ref, wu_ref, wd_ref, g3_ref, b3_ref,
              o_ref, xb_scr, gate_scr, acc_scr):
    e = pl.program_id(1)

    @pl.when(e == 0)
    def _():
        x = x_ref[...]
        xb = x.astype(BF16)
        xb_scr[...] = xb
        gate_scr[...] = jnp.transpose(_route(x, wrt_ref[...], eb_ref[...]))
        hs = jax.nn.silu(_dot(xb, wsg_ref[...])) * _dot(xb, wsu_ref[...])
        acc_scr[...] = _dot(hs.astype(BF16), wsd_ref[...])

    xb = xb_scr[...]
    lane = lax.broadcasted_iota(jnp.int32, gate_scr.shape, 1)
    acc = acc_scr[...]
    for j in range(MOE_EXPERTS_PER_STEP):
        hg = _dot(xb, wg_ref[j].astype(BF16))
        hu = _dot(xb, wu_ref[j].astype(BF16))
        gcol = jnp.sum(jnp.where(lane == e * MOE_EXPERTS_PER_STEP + j, gate_scr[...], 0.0), axis=1, keepdims=True)
        act = jax.nn.silu(hg) * hu * gcol
        acc = acc + _dot(act.astype(BF16), wd_ref[j].astype(BF16))
    acc_scr[...] = acc

    @pl.when(e == pl.num_programs(1) - 1)
    def _():
        o_ref[...] = _layer_norm(DN_ALPHA * x_ref[...] + acc_scr[...], g3_ref[...], b3_ref[...])


def moe_ln(x, wr_t, e_bias, wsg, wsu, wsd, w_e_gate, w_e_up, w_e_down, g3, b3, tm):
    n = x.shape[0]
    eps = MOE_EXPERTS_PER_STEP
    row = pl.BlockSpec((tm, D_MODEL), lambda i, e: (i, 0))
    full = lambda a: pl.BlockSpec(a.shape, lambda i, e: (0,) * a.ndim)
    return pl.pallas_call(
        _moe_body,
        grid=(n // tm, N_EXPERTS // eps),
        in_specs=[row, full(wr_t), full(e_bias), full(wsg), full(wsu), full(wsd),
                  pl.BlockSpec((eps, D_MODEL, D_EXPERT), lambda i, e: (e, 0, 0)),
                  pl.BlockSpec((eps, D_MODEL, D_EXPERT), lambda i, e: (e, 0, 0)),
                  pl.BlockSpec((eps, D_EXPERT, D_MODEL), lambda i, e: (e, 0, 0)),
                  full(g3), full(b3)],
        out_specs=row,
        out_shape=jax.ShapeDtypeStruct((n, D_MODEL), F32),
        scratch_shapes=[pltpu.VMEM((tm, D_MODEL), BF16), pltpu.VMEM((tm, LANES), F32),
                        pltpu.VMEM((tm, D_MODEL), F32)],
        compiler_params=_cparams(("arbitrary", "arbitrary")),
        name="moe_ln3",
    )(x, wr_t, e_bias, wsg, wsu, wsd, w_e_gate, w_e_up, w_e_down, g3, b3)


def _block_ones(seg):
    idx = jnp.arange(LANES, dtype=jnp.int32) // seg
    return (idx[:, None] == idx[None, :]).astype(BF16)


def _row(v):
    return v.reshape(1, -1).astype(F32)


def kernel(x_prompt, x_sample, cache_k, cache_v, state_wkv, state_shift, cache_mem_k, cache_mem_v, page_table, mem_prompt, ln_in_g, ln_in_b, rel_bias, w_in, mu_shift, w0, w_decay_up, a0, w_aaa_up, w_gate_up, k_k, k_a, r_k, lnx_g, lnx_b, lam_q1, lam_k1, lam_q2, lam_k2, subln_g, w_out, ln1_g, ln1_b, w_mq, w_mk, w_mv, w_mo, ln2_g, ln2_b, w_router, e_bias, w_e_gate, w_e_up, w_e_down, w_s_gate, w_s_up, w_s_down, ln3_g, ln3_b):
    assert w_in.shape[0] == DEPTH == 1
    nb, t, _ = x_prompt.shape
    db, s_len, _ = x_sample.shape
    l = 0
    seg64 = _block_ones(RWKV_HD)
    ones128 = _block_ones(LANES)
    zeros = jnp.zeros((64, 512), F32)
    w_lo = jnp.concatenate([jnp.concatenate([w_decay_up[l], zeros], axis=1),
                            jnp.concatenate([zeros, w_aaa_up[l]], axis=1)], axis=0).astype(BF16)
    prep_consts = (_row(mu_shift[l]), w_lo, _row(jnp.concatenate([w0[l], a0[l]])), w_gate_up[l].astype(BF16),
                   _row(k_k[l]), _row(k_a[l]), _row(r_k[l]), seg64)
    mix_consts = (_row(lnx_g[l]), _row(lnx_b[l]), _row(jnp.tile(subln_g[l] * (1.0 - LAM_INIT), N_DIFF_HEADS)),
                  w_out[l].astype(BF16), _row(ln1_g[l]), _row(ln1_b[l]), seg64, ones128, w_mq[l].astype(BF16))
    lam = (jnp.exp(jnp.sum(lam_q1[l] * lam_k1[l]).astype(F32)) - jnp.exp(jnp.sum(lam_q2[l] * lam_k2[l]).astype(F32))
           + LAM_INIT).reshape(1, 1)
    w_in_b = w_in[l].astype(BF16)
    wkt_b = jnp.transpose(w_in_b[:, RWKV_COLS + 512:RWKV_COLS + 1024])
    w_mk_b, w_mv_b, w_mo_b = (w[l].astype(BF16) for w in (w_mk, w_mv, w_mo))
    wr_t = jnp.transpose(w_router[l])
    eb = e_bias[l].reshape(N_EXPERTS, 1).astype(F32)
    wsg, wsu, wsd = w_s_gate[l].astype(BF16), w_s_up[l].astype(BF16), w_s_down[l].astype(BF16)
    g_in, b_in = _row(ln_in_g), _row(ln_in_b)
    g2, b2, g3, b3 = _row(ln2_g[l]), _row(ln2_b[l]), _row(ln3_g[l]), _row(ln3_b[l])

    def rwkv(p, prev, nseq, tlen, s0, tm, tc):
        if prev is None:
            aj, wj, wrj, bj, kj, vt, br, kr, g, bv = rwkv_prep_scan(p, prep_consts, nseq, tlen, LANES)
            jmaj = [aj, wj, wrj, bj, kj]
        else:
            a, w, wr, b, k2, v, br, kr, g, bv = rwkv_prep(p, prev, prep_consts, tm)
            jmaj = [_to_scan_j(z, nseq, tlen) for z in (a, w, wr, b, k2)]
            vt = _to_scan_t(v, nseq, tlen)
        y, st = wkv_scan(*jmaj, vt,
                         _to_scan_s(br, nseq, tlen), _to_scan_s(kr, nseq, tlen), s0, tc)
        return _from_scan_j(y, nseq, tlen), bv, g, _state_from_scan(st, nseq)

    def tail(h1, om, tm, tm_moe):
        h2 = matmul_res_ln(om, h1, w_mo_b, g2, b2, tm, "mem_out_ln2")
        return moe_ln(h2, wr_t, eb, wsg, wsu, wsd, w_e_gate[l], w_e_up[l], w_e_down[l], g3, b3, tm_moe)

    n_p = nb * t
    tm = 512
    hp, pp, qp, kp, vp = ln_proj(x_prompt.reshape(n_p, D_MODEL), g_in, b_in, w_in_b, wkt_b, tm, seq_len=t)
    s0_p = jnp.zeros((nb * N_RWKV_HEADS // SCAN_BH, SCAN_JLO, RWKV_HD, LANES), F32)
    y_p, bv_p, gt_p, wkv_p = rwkv(pp, None, nb, t, s0_p, tm, 128)
    od_p = diff_attn_prompt(qp, kp, vp, rel_bias, lam, nb, t, 512)
    h1_p, qm_p = mix_ln(y_p, bv_p, gt_p, od_p, hp, mix_consts, tm)
    mem_rows = mem_prompt.reshape(nb * N_MEM, D_MODEL)
    mk_p = matmul(mem_rows, w_mk_b, tm, "mem_k")
    mv_p = matmul(mem_rows, w_mv_b, tm, "mem_v")
    om_p = mem_attn(qm_p.reshape(nb, t, D_MODEL), mk_p.reshape(nb, N_MEM, D_MODEL), mv_p.reshape(nb, N_MEM, D_MODEL), tm)
    h3_p = tail(h1_p, om_p.reshape(n_p, D_MODEL), tm, 1024)

    n_s = db * s_len
    hs, ps, qs, ks, vs = ln_proj(x_sample.reshape(n_s, D_MODEL), g_in, b_in, w_in_b, wkt_b, n_s)
    ps3 = ps.reshape(db, s_len, RWKV_COLS)
    prev_s = jnp.concatenate([state_shift[l][:, None, :], ps3[:, :-1]], axis=1).reshape(n_s, RWKV_COLS)
    y_s, bv_s, gt_s, wkv_s = rwkv(ps, prev_s, db, s_len, _state_to_scan(state_wkv[l], db), n_s, s_len)
    n_pool = cache_k.shape[1]
    od_s = diff_attn_sample(qs.reshape(db, s_len, 512), ks.reshape(db, s_len, 512), vs.reshape(db, s_len, 512),
                            jnp.transpose(cache_k[l], (0, 2, 3, 4, 1)).reshape(n_pool, DIFF_WIDTH, PAGE_SIZE),
                            cache_v[l].reshape(n_pool, PAGE_SIZE * N_DIFF_HEADS, DIFF_VD),
                            page_table, rel_bias, lam, 32)
    h1_s, qm_s = mix_ln(y_s, bv_s, gt_s, od_s.reshape(n_s, DIFF_WIDTH), hs, mix_consts, n_s)
    qm_s = jnp.pad(qm_s.reshape(db, s_len, D_MODEL), ((0, 0), (0, SUBLANES - s_len), (0, 0)))
    om_s = mem_attn_cached(qm_s, _mem_cache_rows(cache_mem_k[l]), _mem_cache_rows(cache_mem_v[l]))
    h3_s = tail(h1_s, om_s[:, :s_len].reshape(n_s, D_MODEL), n_s, n_s)

    return (h3_p.reshape(nb, t, D_MODEL), h3_s.reshape(db, s_len, D_MODEL),
            jnp.transpose(kp.reshape(1, nb, N_DIFF_HEADS, 2, DIFF_HD, t), (0, 1, 5, 2, 3, 4)),
            vp.reshape(1, nb, t, N_DIFF_HEADS, DIFF_VD),
            wkv_p[None], pp.reshape(nb, t, RWKV_COLS)[:, -1][None],
            mk_p.reshape(1, nb, N_MEM, N_MEM_HEADS, MEM_HD), mv_p.reshape(1, nb, N_MEM, N_MEM_HEADS, MEM_HD),
            ks.reshape(1, db, s_len, N_DIFF_HEADS, 2, DIFF_HD), vs.reshape(1, db, s_len, N_DIFF_HEADS, DIFF_VD),
            wkv_s[None], ps3[:, -1][None])
```
